```python
import math
import jax, jax.numpy as jnp
from jax import lax
import numpy as np

D_MODEL = 1024
BATCH = 4
SEQ = 4096
DEPTH = 2

D_MIX = 1024
MLA_HEADS = 6
MLA_Q_RANK = 256
MLA_KV_RANK = 128
MLA_NOPE = 64
MLA_ROPE = 32
MLA_V = 64
ROPE_THETA = 10000.0
POOL_WINDOWS = (2, 4, 8, 16)
POOL_GROUP = 64
POOL_WIDTH = POOL_GROUP * len(POOL_WINDOWS)
FOX_HEADS = 6
FOX_HEAD_DIM = 64
FOX_GATE_BIAS_INIT = 2.0
BLOCK_Q = 128
D_FF = 2816
EPS = 1e-6
IN_Q_A = MLA_Q_RANK
IN_KV_A = MLA_KV_RANK
IN_K_ROPE = MLA_ROPE
IN_POOL = POOL_WIDTH
IN_FOX_QKV = 3 * FOX_HEADS * FOX_HEAD_DIM
IN_FOX_F = FOX_HEADS
N_IN = IN_Q_A + IN_KV_A + IN_K_ROPE + IN_POOL + IN_FOX_QKV + IN_FOX_F

kernel_name = "hybrid_mla_pool_fox_macaron"


def rmsnorm(x, g):
    xf = x.astype(jnp.float32)
    y = xf * lax.rsqrt(jnp.mean(xf * xf, axis=-1, keepdims=True) + EPS)
    return y.astype(x.dtype) * g


def rope(x, pos):
    r = x.shape[-1]
    inv_freq = ROPE_THETA ** (-jnp.arange(0, r, 2, dtype=jnp.float32) / r)
    ang = pos.astype(jnp.float32)[:, None] * inv_freq[None, :]
    cos = jnp.cos(ang).astype(x.dtype)
    sin = jnp.sin(ang).astype(x.dtype)
    x1, x2 = x[..., : r // 2], x[..., r // 2:]
    return jnp.concatenate([x1 * cos - x2 * sin, x2 * cos + x1 * sin], axis=-1)


def causal_block_attention(q, k, v, scale, log_decay_cum=None):
    b, h, s, dk = q.shape
    dv = v.shape[-1]
    nb = s // BLOCK_Q
    qb = q.reshape(b, h, nb, BLOCK_Q, dk).transpose(2, 0, 1, 3, 4)
    kpos = jnp.arange(s)
    xs = (jnp.arange(nb), qb)
    if log_decay_cum is not None:
        xs = xs + (log_decay_cum.reshape(b, h, nb, BLOCK_Q).transpose(2, 0, 1, 3),)

    def one_block(args):
        i, q_blk = args[0], args[1]
        sc = jnp.einsum('bhqd,bhkd->bhqk', q_blk, k, preferred_element_type=jnp.float32) * scale
        if log_decay_cum is not None:
            c_blk = args[2]
            sc = sc + c_blk[..., :, None] - log_decay_cum[..., None, :].astype(jnp.float32)
        qpos = i * BLOCK_Q + jnp.arange(BLOCK_Q)
        sc = jnp.where(kpos[None, :] <= qpos[:, None], sc, -jnp.inf)
        p = jax.nn.softmax(sc, axis=-1).astype(v.dtype)
        return jnp.einsum('bhqk,bhkd->bhqd', p, v)

    out = lax.map(one_block, xs)
    return out.transpose(1, 2, 0, 3, 4).reshape(b, h, s, dv)


def mla_mixer(q_a, kv_a, k_rope, q_a_norm, w_q_b, kv_a_norm, w_kv_b, pos):
    b, s, _ = q_a.shape
    q = (rmsnorm(q_a, q_a_norm) @ w_q_b).reshape(b, s, MLA_HEADS, MLA_NOPE + MLA_ROPE).transpose(0, 2, 1, 3)
    q_nope, q_pe = q[..., :MLA_NOPE], rope(q[..., MLA_NOPE:], pos)
    kv = (rmsnorm(kv_a, kv_a_norm) @ w_kv_b).reshape(b, s, MLA_HEADS, MLA_NOPE + MLA_V).transpose(0, 2, 1, 3)
    k_nope, v = kv[..., :MLA_NOPE], kv[..., MLA_NOPE:]
    k_pe = jnp.broadcast_to(rope(k_rope, pos)[:, None], (b, MLA_HEADS, s, MLA_ROPE))
    qf = jnp.concatenate([q_nope, q_pe], axis=-1)
    kf = jnp.concatenate([k_nope, k_pe], axis=-1)
    o = causal_block_attention(qf, kf, v, 1.0 / math.sqrt(MLA_NOPE + MLA_ROPE))
    return o.transpose(0, 2, 1, 3).reshape(b, s, MLA_HEADS * MLA_V)


def pool_mixer(u, pool_w, pool_scale):
    b, s, _ = u.shape
    ng = len(POOL_WINDOWS)
    ug = u.reshape(b, s, ng, POOL_GROUP)
    cs = jnp.cumsum(ug.astype(jnp.float32), axis=1)
    count = jnp.arange(1, s + 1, dtype=jnp.float32)
    means = []
    for g, w in enumerate(POOL_WINDOWS):
        c = cs[:, :, g]
        prev = jnp.pad(c[:, : s - w], ((0, 0), (w, 0), (0, 0)))
        means.append((c - prev) / jnp.minimum(count, float(w))[None, :, None])
    pooled = jnp.stack(means, axis=2).astype(u.dtype) - ug
    y = jnp.einsum('bsgc,gcd->bsgd', pooled, pool_w)
    return y.reshape(b, s, POOL_WIDTH) * pool_scale


def fox_mixer(qkv, f_logit, fox_b_f):
    b, s, _ = qkv.shape
    qkv = qkv.reshape(b, s, 3, FOX_HEADS, FOX_HEAD_DIM).transpose(2, 0, 3, 1, 4)
    q, k, v = qkv[0], qkv[1], qkv[2]
    log_f = jax.nn.log_sigmoid((f_logit + fox_b_f).astype(jnp.float32))
    cum = jnp.cumsum(log_f, axis=1).transpose(0, 2, 1)
    o = causal_block_attention(q, k, v, 1.0 / math.sqrt(FOX_HEAD_DIM), cum)
    return o.transpose(0, 2, 1, 3).reshape(b, s, FOX_HEADS * FOX_HEAD_DIM)


def swiglu(h, w_gu, w_down):
    gu = h @ w_gu
    g, u = gu[..., :D_FF], gu[..., D_FF:]
    return (jax.nn.silu(g) * u) @ w_down


def hybrid_mixing(h, w_in, q_a_norm, w_q_b, kv_a_norm, w_kv_b, pool_w, pool_scale, fox_b_f, w_out, pos):
    z = h @ w_in
    o0 = 0
    o1 = o0 + IN_Q_A
    o2 = o1 + IN_KV_A
    o3 = o2 + IN_K_ROPE
    o4 = o3 + IN_POOL
    o5 = o4 + IN_FOX_QKV
    o6 = o5 + IN_FOX_F
    ya = mla_mixer(z[..., o0:o1], z[..., o1:o2], z[..., o2:o3], q_a_norm, w_q_b, kv_a_norm, w_kv_b, pos)
    yb = pool_mixer(z[..., o3:o4], pool_w, pool_scale)
    yc = fox_mixer(z[..., o4:o5], z[..., o5:o6], fox_b_f)
    return jnp.concatenate([ya, yb, yc], axis=-1) @ w_out


def setup_inputs(seed: int = 0) -> dict:
    key = jax.random.key(seed)
    ks = jax.random.split(key, 24)
    L, D, F = DEPTH, D_MODEL, D_FF
    f32 = jnp.float32

    def nrm(k, shape, fan_in):
        return jax.random.normal(k, shape, f32) * (fan_in ** -0.5)

    def gain(k, shape):
        return 1.0 + 0.02 * jax.random.normal(k, shape, f32)

    return {
        "x": jax.random.normal(ks[0], (BATCH, SEQ, D), f32),
        "ffn1_norm": gain(ks[1], (L, D)),
        "ffn1_w_gu": nrm(ks[2], (L, D, 2 * F), D),
        "ffn1_w_down": nrm(ks[3], (L, F, D), F),
        "mix_norm": gain(ks[4], (L, D)),
        "w_in": nrm(ks[5], (L, D, N_IN), D),
        "q_a_norm": gain(ks[6], (L, MLA_Q_RANK)),
        "w_q_b": nrm(ks[7], (L, MLA_Q_RANK, MLA_HEADS * (MLA_NOPE + MLA_ROPE)), MLA_Q_RANK),
        "kv_a_norm": gain(ks[8], (L, MLA_KV_RANK)),
        "w_kv_b": nrm(ks[9], (L, MLA_KV_RANK, MLA_HEADS * (MLA_NOPE + MLA_V)), MLA_KV_RANK),
        "pool_w": nrm(ks[10], (L, len(POOL_WINDOWS), POOL_GROUP, POOL_GROUP), POOL_GROUP),
        "pool_scale": gain(ks[11], (L, POOL_WIDTH)),
        "fox_b_f": FOX_GATE_BIAS_INIT + 0.5 * jax.random.normal(ks[12], (L, FOX_HEADS), f32),
        "w_out": nrm(ks[13], (L, D_MIX, D), D_MIX),
        "ffn2_norm": gain(ks[14], (L, D)),
        "ffn2_w_gu": nrm(ks[15], (L, D, 2 * F), D),
        "ffn2_w_down": nrm(ks[16], (L, F, D), F),
        "final_norm": gain(ks[17], (D,)),
    }


def reference(x, ffn1_norm, ffn1_w_gu, ffn1_w_down, mix_norm, w_in, q_a_norm, w_q_b, kv_a_norm, w_kv_b,
              pool_w, pool_scale, fox_b_f, w_out, ffn2_norm, ffn2_w_gu, ffn2_w_down, final_norm):
    pos = jnp.arange(x.shape[1], dtype=jnp.int32)
    for l in range(DEPTH):
        x = x + 0.5 * swiglu(rmsnorm(x, ffn1_norm[l]), ffn1_w_gu[l], ffn1_w_down[l])
        x = x + hybrid_mixing(rmsnorm(x, mix_norm[l]), w_in[l], q_a_norm[l], w_q_b[l], kv_a_norm[l], w_kv_b[l],
                              pool_w[l], pool_scale[l], fox_b_f[l], w_out[l], pos)
        x = x + 0.5 * swiglu(rmsnorm(x, ffn2_norm[l]), ffn2_w_gu[l], ffn2_w_down[l])
    return rmsnorm(x, final_norm)
```

```python
import functools
import math

import jax
import jax.numpy as jnp
from jax import lax
from jax.experimental import pallas as pl
from jax.experimental.pallas import tpu as pltpu

F32 = jnp.float32
BF16 = jnp.bfloat16

D_MODEL = 1024
D_FF = 2816
DEPTH = 2
EPS = 1e-6
MLA_HEADS = 6
MLA_Q_RANK = 256
MLA_KV_RANK = 128
MLA_NOPE = 64
MLA_ROPE = 32
MLA_V = 64
ROPE_THETA = 10000.0
POOL_WINDOWS = (2, 4, 8, 16)
POOL_GROUP = 64
POOL_WIDTH = 256
FOX_HEADS = 6
FOX_HEAD_DIM = 64
HEADS = MLA_HEADS + FOX_HEADS
HEAD_PAD = 128
V_DIM = 64
SCALE_MLA = 1.0 / math.sqrt(MLA_NOPE + MLA_ROPE)
SCALE_FOX = 1.0 / math.sqrt(FOX_HEAD_DIM)
POOL_HISTORY = 16
NEG_BIG = -1e30

C_QA = 0
C_KVA = C_QA + MLA_Q_RANK
C_KR = C_KVA + MLA_KV_RANK
C_KRR = C_KR + HEAD_PAD
C_POOL = C_KRR + HEAD_PAD
C_FQ = C_POOL + POOL_WIDTH
C_FK = C_FQ + FOX_HEADS * HEAD_PAD
C_FV = C_FK + FOX_HEADS * HEAD_PAD
C_FF = C_FV + FOX_HEADS * FOX_HEAD_DIM
N_IN_PAD = C_FF + HEAD_PAD

VMEM_LIMIT = 56 * 1024 * 1024

FFN_TM = 512
FFN_FC = 1408
MIX_TS = 512
ATT_T = 512
OUT_TM = 1024


def _rms(x, g):
    return x * lax.rsqrt(jnp.mean(x * x, axis=-1, keepdims=True) + EPS) * g


def _split3(x):
    hi = x.astype(BF16)
    r = x - hi.astype(F32)
    mid = r.astype(BF16)
    lo = (r - mid.astype(F32)).astype(BF16)
    return hi, mid, lo


def _const_spec(shape):
    return pl.BlockSpec(shape, lambda *_: (0,) * len(shape))


def _ffn_kernel(x_ref, g_ref, wg_ref, wu_ref, wd_ref, fin_ref, o_ref, a_ref, *, fc, final):
    x = x_ref[...]
    h = _rms(x, g_ref[...]).astype(BF16)
    for c in range(D_FF // fc):
        sl = slice(c * fc, (c + 1) * fc)
        g = jnp.dot(h, wg_ref[:, sl], preferred_element_type=F32)
        u = jnp.dot(h, wu_ref[:, sl], preferred_element_type=F32)
        a_ref[:, sl] = (g * jax.nn.sigmoid(g) * u).astype(BF16)
    y = x + 0.5 * jnp.dot(a_ref[...], wd_ref[...], preferred_element_type=F32)
    if final:
        y = _rms(y, fin_ref[...])
    o_ref[...] = y


def _ffn(x, g, wg, wu, wd, fin, final):
    t = x.shape[0]
    tm = FFN_TM
    return pl.pallas_call(
        functools.partial(_ffn_kernel, fc=FFN_FC, final=final),
        grid=(t // tm,),
        in_specs=[
            pl.BlockSpec((tm, D_MODEL), lambda i: (i, 0)),
            _const_spec((1, D_MODEL)),
            _const_spec((D_MODEL, D_FF)),
            _const_spec((D_MODEL, D_FF)),
            _const_spec((D_FF, D_MODEL)),
            _const_spec((1, D_MODEL)),
        ],
        out_specs=pl.BlockSpec((tm, D_MODEL), lambda i: (i, 0)),
        out_shape=jax.ShapeDtypeStruct((t, D_MODEL), F32),
        scratch_shapes=[pltpu.VMEM((tm, D_FF), BF16)],
        compiler_params=pltpu.CompilerParams(
            dimension_semantics=("arbitrary",), vmem_limit_bytes=VMEM_LIMIT),
        name="ffn",
    )(x, g, wg, wu, wd, fin)


def _mix_kernel(x_ref, gn_ref, win_ref, qan_ref, wq_ref, kvn_ref, wkv_ref, cos_ref, sin_ref,
                pbd_ref, pscale_ref, fb_ref, e_ref, oq_ref, ok_ref,
                q_out, k_out, v_out, yb_out, hist_ref, carry_ref, *, ts):
    si = pl.program_id(1)

    @pl.when(si == 0)
    def _():
        hist_ref[...] = jnp.zeros_like(hist_ref)
        carry_ref[...] = jnp.zeros_like(carry_ref)

    h = _rms(x_ref[...], gn_ref[...]).astype(BF16)

    def proj(lo, hi):
        return jnp.dot(h, win_ref[:, lo:hi], preferred_element_type=F32)

    cos_t = cos_ref[...]
    sin_t = sin_ref[...]
    nq = MLA_HEADS * HEAD_PAD

    hq = _rms(proj(C_QA, C_KVA), qan_ref[...]).astype(BF16)
    zq = jnp.dot(hq, wq_ref[...], preferred_element_type=F32)
    for hh in range(MLA_HEADS):
        a = zq[:, hh * HEAD_PAD:(hh + 1) * HEAD_PAD]
        b = zq[:, nq + hh * HEAD_PAD:nq + (hh + 1) * HEAD_PAD]
        q_out[:, hh * HEAD_PAD:(hh + 1) * HEAD_PAD] = ((a * cos_t + b * sin_t) * SCALE_MLA).astype(BF16)

    hkv = _rms(proj(C_KVA, C_KR), kvn_ref[...]).astype(BF16)
    zkv = jnp.dot(hkv, wkv_ref[...], preferred_element_type=F32)
    kpe = proj(C_KR, C_KRR) * cos_t + proj(C_KRR, C_POOL) * sin_t
    for hh in range(MLA_HEADS):
        k_out[:, hh * HEAD_PAD:(hh + 1) * HEAD_PAD] = (zkv[:, hh * HEAD_PAD:(hh + 1) * HEAD_PAD] + kpe).astype(BF16)
    v_out[:, 0:MLA_HEADS * V_DIM] = zkv[:, nq:nq + MLA_HEADS * V_DIM].astype(BF16)

    u = proj(C_POOL, C_FQ)
    e = jnp.concatenate([hist_ref[...], u], axis=0)
    s2 = e + pltpu.roll(e, 1, 0)
    s4 = s2 + pltpu.roll(s2, 2, 0)
    s8 = s4 + pltpu.roll(s4, 4, 0)
    s16 = s8 + pltpu.roll(s8, 8, 0)
    hist_ref[...] = u[ts - POOL_HISTORY:, :]
    lane = lax.broadcasted_iota(jnp.int32, (ts, POOL_WIDTH), 1)
    row = lax.broadcasted_iota(jnp.int32, (ts, POOL_WIDTH), 0)
    g_id = lane // POOL_GROUP
    wsum = jnp.where(g_id == 0, s2[POOL_HISTORY:], jnp.where(g_id == 1, s4[POOL_HISTORY:],
                     jnp.where(g_id == 2, s8[POOL_HISTORY:], s16[POOL_HISTORY:])))
    width = jnp.where(g_id == 0, 2, jnp.where(g_id == 1, 4, jnp.where(g_id == 2, 8, 16)))
    count = jnp.minimum(si * ts + row + 1, width).astype(F32)
    pooled = (wsum / count - u).astype(BF16)
    yb = jnp.dot(pooled, pbd_ref[...], preferred_element_type=F32) * pscale_ref[...]
    yb_out[...] = yb.astype(BF16)

    zf = proj(C_FF, N_IN_PAD) + fb_ref[...]
    logf = -(jnp.maximum(-zf, 0.0) + jnp.log1p(jnp.exp(-jnp.abs(zf))))
    r_i = lax.broadcasted_iota(jnp.int32, (ts, ts), 0)
    c_i = lax.broadcasted_iota(jnp.int32, (ts, ts), 1)
    tri = (c_i <= r_i).astype(BF16)
    l_hi, l_mid, l_lo = _split3(logf)
    cum = (jnp.dot(tri, l_hi, preferred_element_type=F32)
           + jnp.dot(tri, l_mid, preferred_element_type=F32)
           + jnp.dot(tri, l_lo, preferred_element_type=F32)) + carry_ref[...]
    carry_ref[...] = cum[ts - 1:ts, :]
    c3 = jnp.concatenate(_split3(cum), axis=1)
    ce = jnp.dot(c3, e_ref[...], preferred_element_type=F32)
    nf = FOX_HEADS * HEAD_PAD
    q_out[:, nq:nq + nf] = (proj(C_FQ, C_FK) * SCALE_FOX + ce[:, :nf] + oq_ref[...]).astype(BF16)
    k_out[:, nq:nq + nf] = (proj(C_FK, C_FV) + ce[:, nf:] + ok_ref[...]).astype(BF16)
    v_out[:, MLA_HEADS * V_DIM:] = proj(C_FV, C_FF).astype(BF16)


def _mix(x, gn, win, qan, wq, kvn, wkv, cos_t, sin_t, pbd, pscale, fb, emat, oq, ok, batch, seq):
    t = x.shape[0]
    ts = MIX_TS
    ns = seq // ts
    row = lambda b, s: (b * ns + s, 0)
    tab = lambda b, s: (s, 0)
    return pl.pallas_call(
        functools.partial(_mix_kernel, ts=ts),
        grid=(batch, ns),
        in_specs=[
            pl.BlockSpec((ts, D_MODEL), row),
            _const_spec((1, D_MODEL)),
            _const_spec((D_MODEL, N_IN_PAD)),
            _const_spec((1, MLA_Q_RANK)),
            _const_spec(wq.shape),
            _const_spec((1, MLA_KV_RANK)),
            _const_spec(wkv.shape),
            pl.BlockSpec((ts, HEAD_PAD), tab),
            pl.BlockSpec((ts, HEAD_PAD), tab),
            _const_spec(pbd.shape),
            _const_spec((1, POOL_WIDTH)),
            _const_spec((1, HEAD_PAD)),
            _const_spec(emat.shape),
            _const_spec(oq.shape),
            _const_spec(ok.shape),
        ],
        out_specs=[
            pl.BlockSpec((ts, HEADS * HEAD_PAD), row),
            pl.BlockSpec((ts, HEADS * HEAD_PAD), row),
            pl.BlockSpec((ts, HEADS * V_DIM), row),
            pl.BlockSpec((ts, POOL_WIDTH), row),
        ],
        out_shape=[
            jax.ShapeDtypeStruct((t, HEADS * HEAD_PAD), BF16),
            jax.ShapeDtypeStruct((t, HEADS * HEAD_PAD), BF16),
            jax.ShapeDtypeStruct((t, HEADS * V_DIM), BF16),
            jax.ShapeDtypeStruct((t, POOL_WIDTH), BF16),
        ],
        scratch_shapes=[pltpu.VMEM((POOL_HISTORY, POOL_WIDTH), F32), pltpu.VMEM((1, HEAD_PAD), F32)],
        compiler_params=pltpu.CompilerParams(
            dimension_semantics=("arbitrary", "arbitrary"), vmem_limit_bytes=VMEM_LIMIT),
        name="mixproj",
    )(x, gn, win, qan, wq, kvn, wkv, cos_t, sin_t, pbd, pscale, fb, emat, oq, ok)


def _attn_kernel(q_ref, k_ref, v_ref, o_ref, *, t):
    i = pl.program_id(2)
    qs = [q_ref[:, hh * HEAD_PAD:(hh + 1) * HEAD_PAD] for hh in range(2)]

    def step(j, carry, masked):
        v = v_ref[pl.ds(j * t, t), :]
        out = []
        for hh in range(2):
            m, l, acc = carry[hh]
            k = k_ref[pl.ds(j * t, t), hh * HEAD_PAD:(hh + 1) * HEAD_PAD]
            s = lax.dot_general(qs[hh], k, (((1,), (1,)), ((), ())), preferred_element_type=F32)
            if masked:
                r_i = lax.broadcasted_iota(jnp.int32, (t, t), 0)
                c_i = lax.broadcasted_iota(jnp.int32, (t, t), 1)
                s = jnp.where(c_i <= r_i, s, NEG_BIG)
            m_new = jnp.maximum(m, jnp.max(s, axis=-1, keepdims=True))
            p = jnp.exp(s - m_new)
            alpha = jnp.exp(m - m_new)
            l = alpha * l + jnp.sum(p, axis=-1, keepdims=True)
            acc = alpha * acc + jnp.dot(p.astype(BF16), v, preferred_element_type=F32)
            out.append((m_new, l, acc))
        return tuple(out)

    init = tuple((jnp.full((t, 1), NEG_BIG, F32), jnp.zeros((t, 1), F32), jnp.zeros((t, 2 * V_DIM), F32))
                 for _ in range(2))
    carry = lax.fori_loop(0, i, lambda j, c: step(j, c, False), init)
    carry = step(i, carry, True)
    lane = lax.broadcasted_iota(jnp.int32, (t, 2 * V_DIM), 1)
    o = jnp.where(lane < V_DIM, carry[0][2] / carry[0][1], carry[1][2] / carry[1][1])
    o_ref[...] = o.astype(BF16)


def _attention(q, k, v, batch, seq):
    t = ATT_T
    nq = seq // t
    pairs = HEADS // 2
    return pl.pallas_call(
        functools.partial(_attn_kernel, t=t),
        grid=(batch, pairs, nq),
        in_specs=[
            pl.BlockSpec((t, 2 * HEAD_PAD), lambda b, p, i: (b * nq + i, p)),
            pl.BlockSpec((seq, 2 * HEAD_PAD), lambda b, p, i: (b, p)),
            pl.BlockSpec((seq, 2 * V_DIM), lambda b, p, i: (b, p)),
        ],
        out_specs=pl.BlockSpec((t, 2 * V_DIM), lambda b, p, i: (b * nq + i, p)),
        out_shape=jax.ShapeDtypeStruct((batch * seq, HEADS * V_DIM), BF16),
        compiler_params=pltpu.CompilerParams(
            dimension_semantics=("arbitrary", "arbitrary", "arbitrary"), vmem_limit_bytes=VMEM_LIMIT),
        name="attention",
    )(q, k, v)


def _out_kernel(x_ref, o_ref, yb_ref, wo_ref, wb_ref, out_ref):
    out_ref[...] = (x_ref[...]
                    + jnp.dot(o_ref[...], wo_ref[...], preferred_element_type=F32)
                    + jnp.dot(yb_ref[...], wb_ref[...], preferred_element_type=F32))


def _outproj(x, o, yb, wo, wb):
    t = x.shape[0]
    tm = OUT_TM
    return pl.pallas_call(
        _out_kernel,
        grid=(t // tm,),
        in_specs=[
            pl.BlockSpec((tm, D_MODEL), lambda i: (i, 0)),
            pl.BlockSpec((tm, HEADS * V_DIM), lambda i: (i, 0)),
            pl.BlockSpec((tm, POOL_WIDTH), lambda i: (i, 0)),
            _const_spec(wo.shape),
            _const_spec(wb.shape),
        ],
        out_specs=pl.BlockSpec((tm, D_MODEL), lambda i: (i, 0)),
        out_shape=jax.ShapeDtypeStruct((t, D_MODEL), F32),
        compiler_params=pltpu.CompilerParams(
            dimension_semantics=("arbitrary",), vmem_limit_bytes=VMEM_LIMIT),
        name="outproj",
    )(x, o, yb, wo, wb)


def _head_blocks(w, n_heads, width):
    k = w.shape[0]
    w = w.reshape(k, n_heads, width)
    return jnp.pad(w, ((0, 0), (0, 0), (0, HEAD_PAD - width))).reshape(k, n_heads * HEAD_PAD)


def _rot_cols(w):
    half = w.shape[-1] // 2
    return jnp.concatenate([-w[..., half:], w[..., :half]], axis=-1)


def _prep_mix_weights(w_in, w_q_b, w_kv_b, pool_w, w_out):
    o1 = MLA_Q_RANK
    o2 = o1 + MLA_KV_RANK
    o3 = o2 + MLA_ROPE
    o4 = o3 + POOL_WIDTH
    o5 = o4 + 3 * FOX_HEADS * FOX_HEAD_DIM
    hd = FOX_HEADS * FOX_HEAD_DIM
    d = w_in.shape[0]
    w_kr = w_in[:, o2:o3]
    pad_rope = lambda w: jnp.pad(w, ((0, 0), (MLA_NOPE, HEAD_PAD - MLA_NOPE - MLA_ROPE)))
    win = jnp.concatenate([
        w_in[:, 0:o2],
        pad_rope(w_kr),
        pad_rope(_rot_cols(w_kr)),
        w_in[:, o3:o4],
        _head_blocks(w_in[:, o4:o4 + hd], FOX_HEADS, FOX_HEAD_DIM),
        _head_blocks(w_in[:, o4 + hd:o4 + 2 * hd], FOX_HEADS, FOX_HEAD_DIM),
        w_in[:, o4 + 2 * hd:o5],
        jnp.pad(w_in[:, o5:], ((0, 0), (0, HEAD_PAD - FOX_HEADS))),
    ], axis=1).astype(BF16)
    assert win.shape == (d, N_IN_PAD)

    wq3 = w_q_b.reshape(MLA_Q_RANK, MLA_HEADS, MLA_NOPE + MLA_ROPE)
    plain = jnp.pad(wq3, ((0, 0), (0, 0), (0, HEAD_PAD - MLA_NOPE - MLA_ROPE)))
    rot = jnp.pad(_rot_cols(wq3[..., MLA_NOPE:]), ((0, 0), (0, 0), (MLA_NOPE, HEAD_PAD - MLA_NOPE - MLA_ROPE)))
    wq = jnp.concatenate([plain.reshape(MLA_Q_RANK, -1), rot.reshape(MLA_Q_RANK, -1)], axis=1).astype(BF16)

    wkv3 = w_kv_b.reshape(MLA_KV_RANK, MLA_HEADS, MLA_NOPE + MLA_V)
    wk = jnp.pad(wkv3[..., :MLA_NOPE], ((0, 0), (0, 0), (0, HEAD_PAD - MLA_NOPE))).reshape(MLA_KV_RANK, -1)
    wv = wkv3[..., MLA_NOPE:].reshape(MLA_KV_RANK, -1)
    wkv = jnp.concatenate([wk, wv], axis=1).astype(BF16)

    pbd = jnp.zeros((POOL_WIDTH, POOL_WIDTH), F32)
    for g in range(len(POOL_WINDOWS)):
        pbd = pbd.at[g * POOL_GROUP:(g + 1) * POOL_GROUP, g * POOL_GROUP:(g + 1) * POOL_GROUP].set(pool_w[g])
    pbd = pbd.astype(BF16)

    na = MLA_HEADS * MLA_V
    wo = jnp.concatenate([w_out[:na], w_out[na + POOL_WIDTH:]], axis=0).astype(BF16)
    wb = w_out[na:na + POOL_WIDTH].astype(BF16)
    return win, wq, wkv, pbd, wo, wb


def _gate_lane_constants():
    nf = FOX_HEADS * HEAD_PAD
    e = [[0.0] * (2 * nf) for _ in range(3 * HEAD_PAD)]
    oq = [0.0] * nf
    ok = [0.0] * nf
    for hh in range(FOX_HEADS):
        for part in range(3):
            e[part * HEAD_PAD + hh][hh * HEAD_PAD + FOX_HEAD_DIM + part] = 1.0
            e[part * HEAD_PAD + hh][nf + hh * HEAD_PAD + FOX_HEAD_DIM + 3 + part] = -1.0
            oq[hh * HEAD_PAD + FOX_HEAD_DIM + 3 + part] = 1.0
            ok[hh * HEAD_PAD + FOX_HEAD_DIM + part] = 1.0
    return (jnp.array(e, F32).astype(BF16), jnp.array(oq, F32).reshape(1, nf), jnp.array(ok, F32).reshape(1, nf))


def _rope_tables(seq):
    r = MLA_ROPE
    inv_freq = ROPE_THETA ** (-jnp.arange(0, r, 2, dtype=F32) / r)
    ang = jnp.arange(seq, dtype=jnp.int32).astype(F32)[:, None] * inv_freq[None, :]
    cos = jnp.cos(ang)
    sin = jnp.sin(ang)
    tail = HEAD_PAD - MLA_NOPE - MLA_ROPE
    cos_t = jnp.concatenate([jnp.ones((seq, MLA_NOPE), F32), cos, cos, jnp.zeros((seq, tail), F32)], axis=1)
    sin_t = jnp.concatenate([jnp.zeros((seq, MLA_NOPE), F32), sin, sin, jnp.zeros((seq, tail), F32)], axis=1)
    return cos_t, sin_t


def kernel(x, ffn1_norm, ffn1_w_gu, ffn1_w_down, mix_norm, w_in, q_a_norm, w_q_b, kv_a_norm, w_kv_b, pool_w, pool_scale, fox_b_f, w_out, ffn2_norm, ffn2_w_gu, ffn2_w_down, final_norm):
    batch, seq, d = x.shape
    xf = x.reshape(batch * seq, d)
    cos_t, sin_t = _rope_tables(seq)
    emat, oq, ok = _gate_lane_constants()
    fin = final_norm.reshape(1, d)
    for l in range(DEPTH):
        xf = _ffn(xf, ffn1_norm[l].reshape(1, d), ffn1_w_gu[l, :, :D_FF].astype(BF16),
                  ffn1_w_gu[l, :, D_FF:].astype(BF16), ffn1_w_down[l].astype(BF16), fin, False)
        win, wq, wkv, pbd, wo, wb = _prep_mix_weights(w_in[l], w_q_b[l], w_kv_b[l], pool_w[l], w_out[l])
        fb = jnp.pad(fox_b_f[l], (0, HEAD_PAD - FOX_HEADS)).reshape(1, HEAD_PAD)
        q, k, v, yb = _mix(xf, mix_norm[l].reshape(1, d), win, q_a_norm[l].reshape(1, -1), wq,
                           kv_a_norm[l].reshape(1, -1), wkv, cos_t, sin_t, pbd,
                           pool_scale[l].reshape(1, -1), fb, emat, oq, ok, batch, seq)
        o = _attention(q, k, v, batch, seq)
        xf = _outproj(xf, o, yb, wo, wb)
        xf = _ffn(xf, ffn2_norm[l].reshape(1, d), ffn2_w_gu[l, :, :D_FF].astype(BF16),
                  ffn2_w_gu[l, :, D_FF:].astype(BF16), ffn2_w_down[l].astype(BF16), fin, l == DEPTH - 1)
    return xf.reshape(batch, seq, d)
```

```python
import functools
import math

import jax
import jax.numpy as jnp
from jax import lax
from jax.experimental import pallas as pl
from jax.experimental.pallas import tpu as pltpu

F32 = jnp.float32
BF16 = jnp.bfloat16

D_MODEL = 1024
D_FF = 2816
DEPTH = 2
EPS = 1e-6
MLA_HEADS = 6
MLA_Q_RANK = 256
MLA_KV_RANK = 128
MLA_NOPE = 64
MLA_ROPE = 32
MLA_V = 64
ROPE_THETA = 10000.0
POOL_WINDOWS = (2, 4, 8, 16)
POOL_GROUP = 64
POOL_WIDTH = 256
FOX_HEADS = 6
FOX_HEAD_DIM = 64
HEADS = MLA_HEADS + FOX_HEADS
HEAD_PAD = 128
V_DIM = 64
LOG2E = math.log2(math.e)
SCALE_MLA = LOG2E / math.sqrt(MLA_NOPE + MLA_ROPE)
SCALE_FOX = LOG2E / math.sqrt(FOX_HEAD_DIM)
POOL_HISTORY = 16
NEG_BIG = -1e30

C_QA = 0
C_KVA = C_QA + MLA_Q_RANK
C_KR = C_KVA + MLA_KV_RANK
C_KRR = C_KR + HEAD_PAD
C_POOL = C_KRR + HEAD_PAD
C_FQ = C_POOL + POOL_WIDTH
C_FK = C_FQ + FOX_HEADS * HEAD_PAD
C_FV = C_FK + FOX_HEADS * HEAD_PAD
C_FF = C_FV + FOX_HEADS * HEAD_PAD
N_IN_PAD = C_FF + HEAD_PAD

VMEM_LIMIT = 56 * 1024 * 1024

FFN_TM = 512
FFN_FC = 1408
MIX_TS = 512
ATT_T = 512
ATT_RC = 64
OUT_TM = 1024


def _rms(x, g):
    return x * lax.rsqrt(jnp.mean(x * x, axis=-1, keepdims=True) + EPS) * g


def _split3(x):
    hi = x.astype(BF16)
    r = x - hi.astype(F32)
    mid = r.astype(BF16)
    lo = (r - mid.astype(F32)).astype(BF16)
    return hi, mid, lo


def _const_spec(shape):
    return pl.BlockSpec(shape, lambda *_: (0,) * len(shape))


def _ffn_kernel(x_ref, g_ref, wg_ref, wu_ref, wd_ref, fin_ref, o_ref, a_ref, *, fc, final):
    x = x_ref[...]
    h = _rms(x, g_ref[...]).astype(BF16)
    for c in range(D_FF // fc):
        sl = slice(c * fc, (c + 1) * fc)
        g = jnp.dot(h, wg_ref[:, sl], preferred_element_type=F32)
        u = jnp.dot(h, wu_ref[:, sl], preferred_element_type=F32)
        a_ref[:, sl] = (g * jax.nn.sigmoid(g) * u).astype(BF16)
    y = x + 0.5 * jnp.dot(a_ref[...], wd_ref[...], preferred_element_type=F32)
    if final:
        y = _rms(y, fin_ref[...])
    o_ref[...] = y


def _ffn(x, g, wg, wu, wd, fin, final):
    t = x.shape[0]
    tm = FFN_TM
    return pl.pallas_call(
        functools.partial(_ffn_kernel, fc=FFN_FC, final=final),
        grid=(t // tm,),
        in_specs=[
            pl.BlockSpec((tm, D_MODEL), lambda i: (i, 0)),
            _const_spec((1, D_MODEL)),
            _const_spec((D_MODEL, D_FF)),
            _const_spec((D_MODEL, D_FF)),
            _const_spec((D_FF, D_MODEL)),
            _const_spec((1, D_MODEL)),
        ],
        out_specs=pl.BlockSpec((tm, D_MODEL), lambda i: (i, 0)),
        out_shape=jax.ShapeDtypeStruct((t, D_MODEL), F32),
        scratch_shapes=[pltpu.VMEM((tm, D_FF), BF16)],
        compiler_params=pltpu.CompilerParams(
            dimension_semantics=("arbitrary",), vmem_limit_bytes=VMEM_LIMIT),
        name="ffn",
    )(x, g, wg, wu, wd, fin)


def _mix_kernel(x_ref, gn_ref, win_ref, qan_ref, wq_ref, kvn_ref, wkv_ref, cos_ref, sin_ref,
                pbd_ref, pscale_ref, fb_ref, e_ref, oq_ref, ok_ref, ov_ref,
                q_out, k_out, v_out, yb_out, hist_ref, carry_ref, *, ts):
    si = pl.program_id(1)

    @pl.when(si == 0)
    def _():
        hist_ref[...] = jnp.zeros_like(hist_ref)
        carry_ref[...] = jnp.zeros_like(carry_ref)

    h = _rms(x_ref[...], gn_ref[...]).astype(BF16)

    def proj(lo, hi):
        return jnp.dot(h, win_ref[:, lo:hi], preferred_element_type=F32)

    cos_t = cos_ref[...]
    sin_t = sin_ref[...]
    nq = MLA_HEADS * HEAD_PAD

    hq = _rms(proj(C_QA, C_KVA), qan_ref[...]).astype(BF16)
    zq = jnp.dot(hq, wq_ref[...], preferred_element_type=F32)
    for hh in range(MLA_HEADS):
        a = zq[:, hh * HEAD_PAD:(hh + 1) * HEAD_PAD]
        b = zq[:, nq + hh * HEAD_PAD:nq + (hh + 1) * HEAD_PAD]
        q_out[:, hh * HEAD_PAD:(hh + 1) * HEAD_PAD] = ((a * cos_t + b * sin_t) * SCALE_MLA).astype(BF16)

    hkv = _rms(proj(C_KVA, C_KR), kvn_ref[...]).astype(BF16)
    zkv = jnp.dot(hkv, wkv_ref[...], preferred_element_type=F32)
    kpe = proj(C_KR, C_KRR) * cos_t + proj(C_KRR, C_POOL) * sin_t
    for hh in range(MLA_HEADS):
        k_out[:, hh * HEAD_PAD:(hh + 1) * HEAD_PAD] = (zkv[:, hh * HEAD_PAD:(hh + 1) * HEAD_PAD] + kpe).astype(BF16)
    ones_v = ov_ref[...]
    v_out[:, 0:nq] = (zkv[:, nq:2 * nq] + ones_v).astype(BF16)

    u = proj(C_POOL, C_FQ)
    e = jnp.concatenate([hist_ref[...], u], axis=0)
    s2 = e + pltpu.roll(e, 1, 0)
    s4 = s2 + pltpu.roll(s2, 2, 0)
    s8 = s4 + pltpu.roll(s4, 4, 0)
    s16 = s8 + pltpu.roll(s8, 8, 0)
    hist_ref[...] = u[ts - POOL_HISTORY:, :]
    lane = lax.broadcasted_iota(jnp.int32, (ts, POOL_WIDTH), 1)
    row = lax.broadcasted_iota(jnp.int32, (ts, POOL_WIDTH), 0)
    g_id = lane // POOL_GROUP
    wsum = jnp.where(g_id == 0, s2[POOL_HISTORY:], jnp.where(g_id == 1, s4[POOL_HISTORY:],
                     jnp.where(g_id == 2, s8[POOL_HISTORY:], s16[POOL_HISTORY:])))
    width = jnp.where(g_id == 0, 2, jnp.where(g_id == 1, 4, jnp.where(g_id == 2, 8, 16)))
    count = jnp.minimum(si * ts + row + 1, width).astype(F32)
    pooled = (wsum / count - u).astype(BF16)
    yb = jnp.dot(pooled, pbd_ref[...], preferred_element_type=F32) * pscale_ref[...]
    yb_out[...] = yb.astype(BF16)

    zf = proj(C_FF, N_IN_PAD) + fb_ref[...]
    logf = -(jnp.maximum(-zf, 0.0) + jnp.log1p(jnp.exp(-jnp.abs(zf))))
    r_i = lax.broadcasted_iota(jnp.int32, (ts, ts), 0)
    c_i = lax.broadcasted_iota(jnp.int32, (ts, ts), 1)
    tri = (c_i <= r_i).astype(BF16)
    l_hi, l_mid, l_lo = _split3(logf)
    cum = (jnp.dot(tri, l_hi, preferred_element_type=F32)
           + jnp.dot(tri, l_mid, preferred_element_type=F32)
           + jnp.dot(tri, l_lo, preferred_element_type=F32)) + carry_ref[...]
    carry_ref[...] = cum[ts - 1:ts, :]
    c3 = jnp.concatenate(_split3(cum * LOG2E), axis=1)
    ce = jnp.dot(c3, e_ref[...], preferred_element_type=F32)
    nf = FOX_HEADS * HEAD_PAD
    q_out[:, nq:nq + nf] = (proj(C_FQ, C_FK) * SCALE_FOX + ce[:, :nf] + oq_ref[...]).astype(BF16)
    k_out[:, nq:nq + nf] = (proj(C_FK, C_FV) + ce[:, nf:] + ok_ref[...]).astype(BF16)
    v_out[:, nq:nq + nf] = (proj(C_FV, C_FF) + ones_v).astype(BF16)


def _mix(x, gn, win, qan, wq, kvn, wkv, cos_t, sin_t, pbd, pscale, fb, emat, oq, ok, ov, batch, seq):
    t = x.shape[0]
    ts = MIX_TS
    ns = seq // ts
    row = lambda b, s: (b * ns + s, 0)
    tab = lambda b, s: (s, 0)
    return pl.pallas_call(
        functools.partial(_mix_kernel, ts=ts),
        grid=(batch, ns),
        in_specs=[
            pl.BlockSpec((ts, D_MODEL), row),
            _const_spec((1, D_MODEL)),
            _const_spec((D_MODEL, N_IN_PAD)),
            _const_spec((1, MLA_Q_RANK)),
            _const_spec(wq.shape),
            _const_spec((1, MLA_KV_RANK)),
            _const_spec(wkv.shape),
            pl.BlockSpec((ts, HEAD_PAD), tab),
            pl.BlockSpec((ts, HEAD_PAD), tab),
            _const_spec(pbd.shape),
            _const_spec((1, POOL_WIDTH)),
            _const_spec((1, HEAD_PAD)),
            _const_spec(emat.shape),
            _const_spec(oq.shape),
            _const_spec(ok.shape),
            _const_spec(ov.shape),
        ],
        out_specs=[
            pl.BlockSpec((ts, HEADS * HEAD_PAD), row),
            pl.BlockSpec((ts, HEADS * HEAD_PAD), row),
            pl.BlockSpec((ts, HEADS * HEAD_PAD), row),
            pl.BlockSpec((ts, POOL_WIDTH), row),
        ],
        out_shape=[
            jax.ShapeDtypeStruct((t, HEADS * HEAD_PAD), BF16),
            jax.ShapeDtypeStruct((t, HEADS * HEAD_PAD), BF16),
            jax.ShapeDtypeStruct((t, HEADS * HEAD_PAD), BF16),
            jax.ShapeDtypeStruct((t, POOL_WIDTH), BF16),
        ],
        scratch_shapes=[pltpu.VMEM((POOL_HISTORY, POOL_WIDTH), F32), pltpu.VMEM((1, HEAD_PAD), F32)],
        compiler_params=pltpu.CompilerParams(
            dimension_semantics=("arbitrary", "arbitrary"), vmem_limit_bytes=VMEM_LIMIT),
        name="mixproj",
    )(x, gn, win, qan, wq, kvn, wkv, cos_t, sin_t, pbd, pscale, fb, emat, oq, ok, ov)


def _attn_kernel(q_ref, k_ref, v_ref, o_ref, s_ref, p_ref, m_ref, acc_ref, *, t, rc):
    i = pl.program_id(2)
    nkb = t // HEAD_PAD
    m_ref[...] = jnp.full(m_ref.shape, NEG_BIG, F32)
    acc_ref[...] = jnp.zeros(acc_ref.shape, F32)

    def step(j, masked):
        for hh in range(2):
            cols = slice(hh * HEAD_PAD, (hh + 1) * HEAD_PAD)
            k = k_ref[pl.ds(j * t, t), cols]
            s_ref[hh] = lax.dot_general(q_ref[:, cols], k, (((1,), (1,)), ((), ())),
                                        preferred_element_type=F32)
            for c in range(t // rc):
                r0 = c * rc
                rows = slice(r0, r0 + rc)
                live = (r0 + rc - 1) // HEAD_PAD + 1 if masked else nkb
                blocks = []
                for kb in range(live):
                    blk = s_ref[hh, rows, kb * HEAD_PAD:(kb + 1) * HEAD_PAD]
                    if masked and (kb + 1) * HEAD_PAD - 1 > r0:
                        r_i = r0 + lax.broadcasted_iota(jnp.int32, (rc, HEAD_PAD), 0)
                        c_i = kb * HEAD_PAD + lax.broadcasted_iota(jnp.int32, (rc, HEAD_PAD), 1)
                        blk = jnp.where(c_i <= r_i, blk, NEG_BIG)
                    blocks.append(blk)
                bmax = blocks[0]
                for blk in blocks[1:]:
                    bmax = jnp.maximum(bmax, blk)
                m_old = m_ref[hh, rows, :]
                m_new = jnp.maximum(m_old, jnp.max(bmax, axis=-1, keepdims=True))
                m_ref[hh, rows, :] = m_new
                acc_ref[hh, rows, :] = jnp.exp2(m_old - m_new) * acc_ref[hh, rows, :]
                for kb in range(nkb):
                    pcols = slice(kb * HEAD_PAD, (kb + 1) * HEAD_PAD)
                    if kb < live:
                        p_ref[hh, rows, pcols] = jnp.exp2(blocks[kb] - m_new).astype(BF16)
                    else:
                        p_ref[hh, rows, pcols] = jnp.zeros((rc, HEAD_PAD), BF16)
            v = v_ref[pl.ds(j * t, t), cols]
            acc_ref[hh] += jnp.dot(p_ref[hh], v, preferred_element_type=F32)

    def body(j, carry):
        step(j, False)
        return carry

    lax.fori_loop(0, i, body, 0)
    step(i, True)
    outs = []
    for hh in range(2):
        acc = acc_ref[hh]
        outs.append(acc[:, :V_DIM] / acc[:, V_DIM:V_DIM + 1])
    o_ref[...] = jnp.concatenate(outs, axis=1).astype(BF16)


def _attention(q, k, v, batch, seq):
    t = ATT_T
    nq = seq // t
    pairs = HEADS // 2
    return pl.pallas_call(
        functools.partial(_attn_kernel, t=t, rc=ATT_RC),
        grid=(batch, pairs, nq),
        in_specs=[
            pl.BlockSpec((t, 2 * HEAD_PAD), lambda b, p, i: (b * nq + i, p)),
            pl.BlockSpec((seq, 2 * HEAD_PAD), lambda b, p, i: (b, p)),
            pl.BlockSpec((seq, 2 * HEAD_PAD), lambda b, p, i: (b, p)),
        ],
        out_specs=pl.BlockSpec((t, 2 * V_DIM), lambda b, p, i: (b * nq + i, p)),
        out_shape=jax.ShapeDtypeStruct((batch * seq, HEADS * V_DIM), BF16),
        scratch_shapes=[
            pltpu.VMEM((2, t, t), F32),
            pltpu.VMEM((2, t, t), BF16),
            pltpu.VMEM((2, t, HEAD_PAD), F32),
            pltpu.VMEM((2, t, HEAD_PAD), F32),
        ],
        compiler_params=pltpu.CompilerParams(
            dimension_semantics=("arbitrary", "arbitrary", "arbitrary"), vmem_limit_bytes=VMEM_LIMIT),
        name="attention",
    )(q, k, v)


def _out_kernel(x_ref, o_ref, yb_ref, wo_ref, wb_ref, out_ref):
    out_ref[...] = (x_ref[...]
                    + jnp.dot(o_ref[...], wo_ref[...], preferred_element_type=F32)
                    + jnp.dot(yb_ref[...], wb_ref[...], preferred_element_type=F32))


def _outproj(x, o, yb, wo, wb):
    t = x.shape[0]
    tm = OUT_TM
    return pl.pallas_call(
        _out_kernel,
        grid=(t // tm,),
        in_specs=[
            pl.BlockSpec((tm, D_MODEL), lambda i: (i, 0)),
            pl.BlockSpec((tm, HEADS * V_DIM), lambda i: (i, 0)),
            pl.BlockSpec((tm, POOL_WIDTH), lambda i: (i, 0)),
            _const_spec(wo.shape),
            _const_spec(wb.shape),
        ],
        out_specs=pl.BlockSpec((tm, D_MODEL), lambda i: (i, 0)),
        out_shape=jax.ShapeDtypeStruct((t, D_MODEL), F32),
        compiler_params=pltpu.CompilerParams(
            dimension_semantics=("arbitrary",), vmem_limit_bytes=VMEM_LIMIT),
        name="outproj",
    )(x, o, yb, wo, wb)


def _head_blocks(w, n_heads, width):
    k = w.shape[0]
    w = w.reshape(k, n_heads, width)
    return jnp.pad(w, ((0, 0), (0, 0), (0, HEAD_PAD - width))).reshape(k, n_heads * HEAD_PAD)


def _rot_cols(w):
    half = w.shape[-1] // 2
    return jnp.concatenate([-w[..., half:], w[..., :half]], axis=-1)


def _prep_mix_weights(w_in, w_q_b, w_kv_b, pool_w, w_out):
    o1 = MLA_Q_RANK
    o2 = o1 + MLA_KV_RANK
    o3 = o2 + MLA_ROPE
    o4 = o3 + POOL_WIDTH
    o5 = o4 + 3 * FOX_HEADS * FOX_HEAD_DIM
    hd = FOX_HEADS * FOX_HEAD_DIM
    d = w_in.shape[0]
    w_kr = w_in[:, o2:o3]
    pad_rope = lambda w: jnp.pad(w, ((0, 0), (MLA_NOPE, HEAD_PAD - MLA_NOPE - MLA_ROPE)))
    win = jnp.concatenate([
        w_in[:, 0:o2],
        pad_rope(w_kr),
        pad_rope(_rot_cols(w_kr)),
        w_in[:, o3:o4],
        _head_blocks(w_in[:, o4:o4 + hd], FOX_HEADS, FOX_HEAD_DIM),
        _head_blocks(w_in[:, o4 + hd:o4 + 2 * hd], FOX_HEADS, FOX_HEAD_DIM),
        _head_blocks(w_in[:, o4 + 2 * hd:o5], FOX_HEADS, FOX_HEAD_DIM),
        jnp.pad(w_in[:, o5:], ((0, 0), (0, HEAD_PAD - FOX_HEADS))),
    ], axis=1).astype(BF16)
    assert win.shape == (d, N_IN_PAD)

    wq3 = w_q_b.reshape(MLA_Q_RANK, MLA_HEADS, MLA_NOPE + MLA_ROPE)
    plain = jnp.pad(wq3, ((0, 0), (0, 0), (0, HEAD_PAD - MLA_NOPE - MLA_ROPE)))
    rot = jnp.pad(_rot_cols(wq3[..., MLA_NOPE:]), ((0, 0), (0, 0), (MLA_NOPE, HEAD_PAD - MLA_NOPE - MLA_ROPE)))
    wq = jnp.concatenate([plain.reshape(MLA_Q_RANK, -1), rot.reshape(MLA_Q_RANK, -1)], axis=1).astype(BF16)

    wkv3 = w_kv_b.reshape(MLA_KV_RANK, MLA_HEADS, MLA_NOPE + MLA_V)
    wk = jnp.pad(wkv3[..., :MLA_NOPE], ((0, 0), (0, 0), (0, HEAD_PAD - MLA_NOPE))).reshape(MLA_KV_RANK, -1)
    wv = jnp.pad(wkv3[..., MLA_NOPE:], ((0, 0), (0, 0), (0, HEAD_PAD - MLA_V))).reshape(MLA_KV_RANK, -1)
    wkv = jnp.concatenate([wk, wv], axis=1).astype(BF16)

    pbd = jnp.zeros((POOL_WIDTH, POOL_WIDTH), F32)
    for g in range(len(POOL_WINDOWS)):
        pbd = pbd.at[g * POOL_GROUP:(g + 1) * POOL_GROUP, g * POOL_GROUP:(g + 1) * POOL_GROUP].set(pool_w[g])
    pbd = pbd.astype(BF16)

    na = MLA_HEADS * MLA_V
    wo = jnp.concatenate([w_out[:na], w_out[na + POOL_WIDTH:]], axis=0).astype(BF16)
    wb = w_out[na:na + POOL_WIDTH].astype(BF16)
    return win, wq, wkv, pbd, wo, wb


def _gate_lane_constants():
    nf = FOX_HEADS * HEAD_PAD
    e = [[0.0] * (2 * nf) for _ in range(3 * HEAD_PAD)]
    oq = [0.0] * nf
    ok = [0.0] * nf
    ov = [0.0] * nf
    for hh in range(FOX_HEADS):
        ov[hh * HEAD_PAD + V_DIM] = 1.0
        for part in range(3):
            e[part * HEAD_PAD + hh][hh * HEAD_PAD + FOX_HEAD_DIM + part] = 1.0
            e[part * HEAD_PAD + hh][nf + hh * HEAD_PAD + FOX_HEAD_DIM + 3 + part] = -1.0
            oq[hh * HEAD_PAD + FOX_HEAD_DIM + 3 + part] = 1.0
            ok[hh * HEAD_PAD + FOX_HEAD_DIM + part] = 1.0
    row = lambda r: jnp.array(r, F32).reshape(1, nf)
    return jnp.array(e, F32).astype(BF16), row(oq), row(ok), row(ov)


def _rope_tables(seq):
    r = MLA_ROPE
    inv_freq = ROPE_THETA ** (-jnp.arange(0, r, 2, dtype=F32) / r)
    ang = jnp.arange(seq, dtype=jnp.int32).astype(F32)[:, None] * inv_freq[None, :]
    cos = jnp.cos(ang)
    sin = jnp.sin(ang)
    tail = HEAD_PAD - MLA_NOPE - MLA_ROPE
    cos_t = jnp.concatenate([jnp.ones((seq, MLA_NOPE), F32), cos, cos, jnp.zeros((seq, tail), F32)], axis=1)
    sin_t = jnp.concatenate([jnp.zeros((seq, MLA_NOPE), F32), sin, sin, jnp.zeros((seq, tail), F32)], axis=1)
    return cos_t, sin_t


def kernel(x, ffn1_norm, ffn1_w_gu, ffn1_w_down, mix_norm, w_in, q_a_norm, w_q_b, kv_a_norm, w_kv_b, pool_w, pool_scale, fox_b_f, w_out, ffn2_norm, ffn2_w_gu, ffn2_w_down, final_norm):
    batch, seq, d = x.shape
    xf = x.reshape(batch * seq, d)
    cos_t, sin_t = _rope_tables(seq)
    emat, oq, ok, ov = _gate_lane_constants()
    fin = final_norm.reshape(1, d)
    for l in range(DEPTH):
        xf = _ffn(xf, ffn1_norm[l].reshape(1, d), ffn1_w_gu[l, :, :D_FF].astype(BF16),
                  ffn1_w_gu[l, :, D_FF:].astype(BF16), ffn1_w_down[l].astype(BF16), fin, False)
        win, wq, wkv, pbd, wo, wb = _prep_mix_weights(w_in[l], w_q_b[l], w_kv_b[l], pool_w[l], w_out[l])
        fb = jnp.pad(fox_b_f[l], (0, HEAD_PAD - FOX_HEADS)).reshape(1, HEAD_PAD)
        q, k, v, yb = _mix(xf, mix_norm[l].reshape(1, d), win, q_a_norm[l].reshape(1, -1), wq,
                           kv_a_norm[l].reshape(1, -1), wkv, cos_t, sin_t, pbd,
                           pool_scale[l].reshape(1, -1), fb, emat, oq, ok, ov, batch, seq)
        o = _attention(q, k, v, batch, seq)
        xf = _outproj(xf, o, yb, wo, wb)
        xf = _ffn(xf, ffn2_norm[l].reshape(1, d), ffn2_w_gu[l, :, :D_FF].astype(BF16),
                  ffn2_w_gu[l, :, D_FF:].astype(BF16), ffn2_w_down[l].astype(BF16), fin, l == DEPTH - 1)
    return xf.reshape(batch, seq, d)
```

```python
import functools
import math

import jax
import jax.numpy as jnp
from jax import lax
from jax.experimental import pallas as pl
from jax.experimental.pallas import tpu as pltpu

F32 = jnp.float32
BF16 = jnp.bfloat16

D_MODEL = 1024
D_FF = 2816
DEPTH = 2
EPS = 1e-6
MLA_HEADS = 6
MLA_Q_RANK = 256
MLA_KV_RANK = 128
MLA_NOPE = 64
MLA_ROPE = 32
MLA_V = 64
ROPE_THETA = 10000.0
POOL_WINDOWS = (2, 4, 8, 16)
POOL_GROUP = 64
POOL_WIDTH = 256
FOX_HEADS = 6
FOX_HEAD_DIM = 64
HEADS = MLA_HEADS + FOX_HEADS
HEAD_PAD = 128
V_DIM = 64
LOG2E = math.log2(math.e)
SCALE_MLA = LOG2E / math.sqrt(MLA_NOPE + MLA_ROPE)
SCALE_FOX = LOG2E / math.sqrt(FOX_HEAD_DIM)
POOL_HISTORY = 16
NEG_BIG = -1e30

C_QA = 0
C_KVA = C_QA + MLA_Q_RANK
C_KR = C_KVA + MLA_KV_RANK
C_KRR = C_KR + HEAD_PAD
C_POOL = C_KRR + HEAD_PAD
C_FQ = C_POOL + POOL_WIDTH
C_FK = C_FQ + FOX_HEADS * HEAD_PAD
C_FV = C_FK + FOX_HEADS * HEAD_PAD
C_FF = C_FV + FOX_HEADS * HEAD_PAD
N_IN_PAD = C_FF + HEAD_PAD

VMEM_LIMIT = 56 * 1024 * 1024

FFN_TM = 512
FFN_FC = 1536
MIX_TS = 512
ATT_T = 512
ATT_RC = 64
OUT_TM = 1024


def _rms(x, g):
    return x * lax.rsqrt(jnp.mean(x * x, axis=-1, keepdims=True) + EPS) * g


def _split3(x):
    hi = x.astype(BF16)
    r = x - hi.astype(F32)
    mid = r.astype(BF16)
    lo = (r - mid.astype(F32)).astype(BF16)
    return hi, mid, lo


def _const_spec(shape):
    return pl.BlockSpec(shape, lambda *_: (0,) * len(shape))


def _ffn_kernel(x_ref, g_ref, wg_ref, wu_ref, wd_ref, fin_ref, o_ref, a_ref, *, fc, final):
    x = x_ref[...]
    h = _rms(x, g_ref[...]).astype(BF16)
    for lo in range(0, D_FF, fc):
        sl = slice(lo, min(lo + fc, D_FF))
        g = jnp.dot(h, wg_ref[:, sl], preferred_element_type=F32)
        u = jnp.dot(h, wu_ref[:, sl], preferred_element_type=F32)
        a_ref[:, sl] = (g * jax.nn.sigmoid(g) * u).astype(BF16)
    y = x + 0.5 * jnp.dot(a_ref[...], wd_ref[...], preferred_element_type=F32)
    if final:
        y = _rms(y, fin_ref[...])
    o_ref[...] = y


def _ffn(x, g, wg, wu, wd, fin, final):
    t = x.shape[0]
    tm = FFN_TM
    return pl.pallas_call(
        functools.partial(_ffn_kernel, fc=FFN_FC, final=final),
        grid=(t // tm,),
        in_specs=[
            pl.BlockSpec((tm, D_MODEL), lambda i: (i, 0)),
            _const_spec((1, D_MODEL)),
            _const_spec((D_MODEL, D_FF)),
            _const_spec((D_MODEL, D_FF)),
            _const_spec((D_FF, D_MODEL)),
            _const_spec((1, D_MODEL)),
        ],
        out_specs=pl.BlockSpec((tm, D_MODEL), lambda i: (i, 0)),
        out_shape=jax.ShapeDtypeStruct((t, D_MODEL), F32),
        scratch_shapes=[pltpu.VMEM((tm, D_FF), BF16)],
        compiler_params=pltpu.CompilerParams(
            dimension_semantics=("arbitrary",), vmem_limit_bytes=VMEM_LIMIT),
        name="ffn",
    )(x, g, wg, wu, wd, fin)


def _mix_kernel(x_ref, gn_ref, win_ref, qan_ref, wq_ref, kvn_ref, wkv_ref, cos_ref, sin_ref,
                pbd_ref, pscale_ref, fb_ref, e_ref, oq_ref, ok_ref, ov_ref,
                q_out, k_out, v_out, yb_out, hist_ref, carry_ref, *, ts):
    si = pl.program_id(1)

    @pl.when(si == 0)
    def _():
        hist_ref[...] = jnp.zeros_like(hist_ref)
        carry_ref[...] = jnp.zeros_like(carry_ref)

    h = _rms(x_ref[...], gn_ref[...]).astype(BF16)

    def proj(lo, hi):
        return jnp.dot(h, win_ref[:, lo:hi], preferred_element_type=F32)

    cos_t = cos_ref[...]
    sin_t = sin_ref[...]
    nq = MLA_HEADS * HEAD_PAD

    hq = _rms(proj(C_QA, C_KVA), qan_ref[...]).astype(BF16)
    zq = jnp.dot(hq, wq_ref[...], preferred_element_type=F32)
    for hh in range(MLA_HEADS):
        a = zq[:, hh * HEAD_PAD:(hh + 1) * HEAD_PAD]
        b = zq[:, nq + hh * HEAD_PAD:nq + (hh + 1) * HEAD_PAD]
        q_out[:, hh * HEAD_PAD:(hh + 1) * HEAD_PAD] = ((a * cos_t + b * sin_t) * SCALE_MLA).astype(BF16)

    hkv = _rms(proj(C_KVA, C_KR), kvn_ref[...]).astype(BF16)
    zkv = jnp.dot(hkv, wkv_ref[...], preferred_element_type=F32)
    kpe = proj(C_KR, C_KRR) * cos_t + proj(C_KRR, C_POOL) * sin_t
    for hh in range(MLA_HEADS):
        k_out[:, hh * HEAD_PAD:(hh + 1) * HEAD_PAD] = (zkv[:, hh * HEAD_PAD:(hh + 1) * HEAD_PAD] + kpe).astype(BF16)
    ones_v = ov_ref[...]
    v_out[:, 0:nq] = (zkv[:, nq:2 * nq] + ones_v).astype(BF16)

    u = proj(C_POOL, C_FQ)
    e = jnp.concatenate([hist_ref[...], u], axis=0)
    s2 = e + pltpu.roll(e, 1, 0)
    s4 = s2 + pltpu.roll(s2, 2, 0)
    s8 = s4 + pltpu.roll(s4, 4, 0)
    s16 = s8 + pltpu.roll(s8, 8, 0)
    hist_ref[...] = u[ts - POOL_HISTORY:, :]
    lane = lax.broadcasted_iota(jnp.int32, (ts, POOL_WIDTH), 1)
    row = lax.broadcasted_iota(jnp.int32, (ts, POOL_WIDTH), 0)
    g_id = lane // POOL_GROUP
    wsum = jnp.where(g_id == 0, s2[POOL_HISTORY:], jnp.where(g_id == 1, s4[POOL_HISTORY:],
                     jnp.where(g_id == 2, s8[POOL_HISTORY:], s16[POOL_HISTORY:])))
    width = jnp.where(g_id == 0, 2, jnp.where(g_id == 1, 4, jnp.where(g_id == 2, 8, 16)))
    count = jnp.minimum(si * ts + row + 1, width).astype(F32)
    pooled = (wsum / count - u).astype(BF16)
    yb = jnp.dot(pooled, pbd_ref[...], preferred_element_type=F32) * pscale_ref[...]
    yb_out[...] = yb.astype(BF16)

    zf = proj(C_FF, N_IN_PAD) + fb_ref[...]
    logf = -(jnp.maximum(-zf, 0.0) + jnp.log1p(jnp.exp(-jnp.abs(zf))))
    r_i = lax.broadcasted_iota(jnp.int32, (ts, ts), 0)
    c_i = lax.broadcasted_iota(jnp.int32, (ts, ts), 1)
    tri = (c_i <= r_i).astype(BF16)
    l_hi, l_mid, l_lo = _split3(logf)
    cum = (jnp.dot(tri, l_hi, preferred_element_type=F32)
           + jnp.dot(tri, l_mid, preferred_element_type=F32)
           + jnp.dot(tri, l_lo, preferred_element_type=F32)) + carry_ref[...]
    carry_ref[...] = cum[ts - 1:ts, :]
    c3 = jnp.concatenate(_split3(cum * LOG2E), axis=1)
    ce = jnp.dot(c3, e_ref[...], preferred_element_type=F32)
    nf = FOX_HEADS * HEAD_PAD
    q_out[:, nq:nq + nf] = (proj(C_FQ, C_FK) * SCALE_FOX + ce[:, :nf] + oq_ref[...]).astype(BF16)
    k_out[:, nq:nq + nf] = (proj(C_FK, C_FV) + ce[:, nf:] + ok_ref[...]).astype(BF16)
    v_out[:, nq:nq + nf] = (proj(C_FV, C_FF) + ones_v).astype(BF16)


def _mix(x, gn, win, qan, wq, kvn, wkv, cos_t, sin_t, pbd, pscale, fb, emat, oq, ok, ov, batch, seq):
    t = x.shape[0]
    ts = MIX_TS
    ns = seq // ts
    row = lambda b, s: (b * ns + s, 0)
    tab = lambda b, s: (s, 0)
    return pl.pallas_call(
        functools.partial(_mix_kernel, ts=ts),
        grid=(batch, ns),
        in_specs=[
            pl.BlockSpec((ts, D_MODEL), row),
            _const_spec((1, D_MODEL)),
            _const_spec((D_MODEL, N_IN_PAD)),
            _const_spec((1, MLA_Q_RANK)),
            _const_spec(wq.shape),
            _const_spec((1, MLA_KV_RANK)),
            _const_spec(wkv.shape),
            pl.BlockSpec((ts, HEAD_PAD), tab),
            pl.BlockSpec((ts, HEAD_PAD), tab),
            _const_spec(pbd.shape),
            _const_spec((1, POOL_WIDTH)),
            _const_spec((1, HEAD_PAD)),
            _const_spec(emat.shape),
            _const_spec(oq.shape),
            _const_spec(ok.shape),
            _const_spec(ov.shape),
        ],
        out_specs=[
            pl.BlockSpec((ts, HEADS * HEAD_PAD), row),
            pl.BlockSpec((ts, HEADS * HEAD_PAD), row),
            pl.BlockSpec((ts, HEADS * HEAD_PAD), row),
            pl.BlockSpec((ts, POOL_WIDTH), row),
        ],
        out_shape=[
            jax.ShapeDtypeStruct((t, HEADS * HEAD_PAD), BF16),
            jax.ShapeDtypeStruct((t, HEADS * HEAD_PAD), BF16),
            jax.ShapeDtypeStruct((t, HEADS * HEAD_PAD), BF16),
            jax.ShapeDtypeStruct((t, POOL_WIDTH), BF16),
        ],
        scratch_shapes=[pltpu.VMEM((POOL_HISTORY, POOL_WIDTH), F32), pltpu.VMEM((1, HEAD_PAD), F32)],
        compiler_params=pltpu.CompilerParams(
            dimension_semantics=("arbitrary", "arbitrary"), vmem_limit_bytes=VMEM_LIMIT),
        name="mixproj",
    )(x, gn, win, qan, wq, kvn, wkv, cos_t, sin_t, pbd, pscale, fb, emat, oq, ok, ov)


def _attn_kernel(q_ref, k_ref, v_ref, o_ref, sa_ref, sb_ref, pa_ref, pb_ref, m_ref, acc_ref, *, t, rc):
    i = pl.program_id(2)
    nkb = t // HEAD_PAD
    m_ref[...] = jnp.full(m_ref.shape, NEG_BIG, F32)
    acc_ref[...] = jnp.zeros(acc_ref.shape, F32)

    def scores(s_ref, j):
        for hh in range(2):
            cols = slice(hh * HEAD_PAD, (hh + 1) * HEAD_PAD)
            k = k_ref[pl.ds(j * t, t), cols]
            s_ref[hh] = lax.dot_general(q_ref[:, cols], k, (((1,), (1,)), ((), ())),
                                        preferred_element_type=F32)

    def softmax_pv(s_ref, p_ref, j, masked):
        for hh in range(2):
            cols = slice(hh * HEAD_PAD, (hh + 1) * HEAD_PAD)
            for c in range(t // rc):
                r0 = c * rc
                rows = slice(r0, r0 + rc)
                live = (r0 + rc - 1) // HEAD_PAD + 1 if masked else nkb
                blocks = []
                for kb in range(live):
                    blk = s_ref[hh, rows, kb * HEAD_PAD:(kb + 1) * HEAD_PAD]
                    if masked and (kb + 1) * HEAD_PAD - 1 > r0:
                        r_i = r0 + lax.broadcasted_iota(jnp.int32, (rc, HEAD_PAD), 0)
                        c_i = kb * HEAD_PAD + lax.broadcasted_iota(jnp.int32, (rc, HEAD_PAD), 1)
                        blk = jnp.where(c_i <= r_i, blk, NEG_BIG)
                    blocks.append(blk)
                bmax = blocks[0]
                for blk in blocks[1:]:
                    bmax = jnp.maximum(bmax, blk)
                m_old = m_ref[hh, rows, :]
                m_new = jnp.maximum(m_old, jnp.max(bmax, axis=-1, keepdims=True))
                m_ref[hh, rows, :] = m_new
                acc_ref[hh, rows, :] = jnp.exp2(m_old - m_new) * acc_ref[hh, rows, :]
                for kb in range(nkb):
                    pcols = slice(kb * HEAD_PAD, (kb + 1) * HEAD_PAD)
                    if kb < live:
                        p_ref[hh, rows, pcols] = jnp.exp2(blocks[kb] - m_new).astype(BF16)
                    else:
                        p_ref[hh, rows, pcols] = jnp.zeros((rc, HEAD_PAD), BF16)
            v = v_ref[pl.ds(j * t, t), cols]
            acc_ref[hh] += jnp.dot(p_ref[hh], v, preferred_element_type=F32)

    scores(sa_ref, 0)

    def body(jj, carry):
        j = 2 * jj
        scores(sb_ref, j + 1)
        softmax_pv(sa_ref, pa_ref, j, False)
        scores(sa_ref, j + 2)
        softmax_pv(sb_ref, pb_ref, j + 1, False)
        return carry

    lax.fori_loop(0, i // 2, body, 0)

    @pl.when(i % 2 == 0)
    def _():
        softmax_pv(sa_ref, pa_ref, i, True)

    @pl.when(i % 2 == 1)
    def _():
        scores(sb_ref, i)
        softmax_pv(sa_ref, pa_ref, i - 1, False)
        softmax_pv(sb_ref, pb_ref, i, True)

    outs = []
    for hh in range(2):
        acc = acc_ref[hh]
        outs.append(acc[:, :V_DIM] / acc[:, V_DIM:V_DIM + 1])
    o_ref[...] = jnp.concatenate(outs, axis=1).astype(BF16)


def _attention(q, k, v, batch, seq):
    t = ATT_T
    nq = seq // t
    pairs = HEADS // 2
    return pl.pallas_call(
        functools.partial(_attn_kernel, t=t, rc=ATT_RC),
        grid=(batch, pairs, nq),
        in_specs=[
            pl.BlockSpec((t, 2 * HEAD_PAD), lambda b, p, i: (b * nq + i, p)),
            pl.BlockSpec((seq, 2 * HEAD_PAD), lambda b, p, i: (b, p)),
            pl.BlockSpec((seq, 2 * HEAD_PAD), lambda b, p, i: (b, p)),
        ],
        out_specs=pl.BlockSpec((t, 2 * V_DIM), lambda b, p, i: (b * nq + i, p)),
        out_shape=jax.ShapeDtypeStruct((batch * seq, HEADS * V_DIM), BF16),
        scratch_shapes=[
            pltpu.VMEM((2, t, t), F32),
            pltpu.VMEM((2, t, t), F32),
            pltpu.VMEM((2, t, t), BF16),
            pltpu.VMEM((2, t, t), BF16),
            pltpu.VMEM((2, t, HEAD_PAD), F32),
            pltpu.VMEM((2, t, HEAD_PAD), F32),
        ],
        compiler_params=pltpu.CompilerParams(
            dimension_semantics=("arbitrary", "arbitrary", "arbitrary"), vmem_limit_bytes=VMEM_LIMIT),
        name="attention",
    )(q, k, v)


def _out_kernel(x_ref, o_ref, yb_ref, wo_ref, wb_ref, out_ref):
    out_ref[...] = (x_ref[...]
                    + jnp.dot(o_ref[...], wo_ref[...], preferred_element_type=F32)
                    + jnp.dot(yb_ref[...], wb_ref[...], preferred_element_type=F32))


def _outproj(x, o, yb, wo, wb):
    t = x.shape[0]
    tm = OUT_TM
    return pl.pallas_call(
        _out_kernel,
        grid=(t // tm,),
        in_specs=[
            pl.BlockSpec((tm, D_MODEL), lambda i: (i, 0)),
            pl.BlockSpec((tm, HEADS * V_DIM), lambda i: (i, 0)),
            pl.BlockSpec((tm, POOL_WIDTH), lambda i: (i, 0)),
            _const_spec(wo.shape),
            _const_spec(wb.shape),
        ],
        out_specs=pl.BlockSpec((tm, D_MODEL), lambda i: (i, 0)),
        out_shape=jax.ShapeDtypeStruct((t, D_MODEL), F32),
        compiler_params=pltpu.CompilerParams(
            dimension_semantics=("arbitrary",), vmem_limit_bytes=VMEM_LIMIT),
        name="outproj",
    )(x, o, yb, wo, wb)


def _head_blocks(w, n_heads, width):
    k = w.shape[0]
    w = w.reshape(k, n_heads, width)
    return jnp.pad(w, ((0, 0), (0, 0), (0, HEAD_PAD - width))).reshape(k, n_heads * HEAD_PAD)


def _rot_cols(w):
    half = w.shape[-1] // 2
    return jnp.concatenate([-w[..., half:], w[..., :half]], axis=-1)


def _prep_mix_weights(w_in, w_q_b, w_kv_b, pool_w, w_out):
    o1 = MLA_Q_RANK
    o2 = o1 + MLA_KV_RANK
    o3 = o2 + MLA_ROPE
    o4 = o3 + POOL_WIDTH
    o5 = o4 + 3 * FOX_HEADS * FOX_HEAD_DIM
    hd = FOX_HEADS * FOX_HEAD_DIM
    d = w_in.shape[0]
    w_kr = w_in[:, o2:o3]
    pad_rope = lambda w: jnp.pad(w, ((0, 0), (MLA_NOPE, HEAD_PAD - MLA_NOPE - MLA_ROPE)))
    win = jnp.concatenate([
        w_in[:, 0:o2],
        pad_rope(w_kr),
        pad_rope(_rot_cols(w_kr)),
        w_in[:, o3:o4],
        _head_blocks(w_in[:, o4:o4 + hd], FOX_HEADS, FOX_HEAD_DIM),
        _head_blocks(w_in[:, o4 + hd:o4 + 2 * hd], FOX_HEADS, FOX_HEAD_DIM),
        _head_blocks(w_in[:, o4 + 2 * hd:o5], FOX_HEADS, FOX_HEAD_DIM),
        jnp.pad(w_in[:, o5:], ((0, 0), (0, HEAD_PAD - FOX_HEADS))),
    ], axis=1).astype(BF16)
    assert win.shape == (d, N_IN_PAD)

    wq3 = w_q_b.reshape(MLA_Q_RANK, MLA_HEADS, MLA_NOPE + MLA_ROPE)
    plain = jnp.pad(wq3, ((0, 0), (0, 0), (0, HEAD_PAD - MLA_NOPE - MLA_ROPE)))
    rot = jnp.pad(_rot_cols(wq3[..., MLA_NOPE:]), ((0, 0), (0, 0), (MLA_NOPE, HEAD_PAD - MLA_NOPE - MLA_ROPE)))
    wq = jnp.concatenate([plain.reshape(MLA_Q_RANK, -1), rot.reshape(MLA_Q_RANK, -1)], axis=1).astype(BF16)

    wkv3 = w_kv_b.reshape(MLA_KV_RANK, MLA_HEADS, MLA_NOPE + MLA_V)
    wk = jnp.pad(wkv3[..., :MLA_NOPE], ((0, 0), (0, 0), (0, HEAD_PAD - MLA_NOPE))).reshape(MLA_KV_RANK, -1)
    wv = jnp.pad(wkv3[..., MLA_NOPE:], ((0, 0), (0, 0), (0, HEAD_PAD - MLA_V))).reshape(MLA_KV_RANK, -1)
    wkv = jnp.concatenate([wk, wv], axis=1).astype(BF16)

    pbd = jnp.zeros((POOL_WIDTH, POOL_WIDTH), F32)
    for g in range(len(POOL_WINDOWS)):
        pbd = pbd.at[g * POOL_GROUP:(g + 1) * POOL_GROUP, g * POOL_GROUP:(g + 1) * POOL_GROUP].set(pool_w[g])
    pbd = pbd.astype(BF16)

    na = MLA_HEADS * MLA_V
    wo = jnp.concatenate([w_out[:na], w_out[na + POOL_WIDTH:]], axis=0).astype(BF16)
    wb = w_out[na:na + POOL_WIDTH].astype(BF16)
    return win, wq, wkv, pbd, wo, wb


def _gate_lane_constants():
    nf = FOX_HEADS * HEAD_PAD
    e = [[0.0] * (2 * nf) for _ in range(3 * HEAD_PAD)]
    oq = [0.0] * nf
    ok = [0.0] * nf
    ov = [0.0] * nf
    for hh in range(FOX_HEADS):
        ov[hh * HEAD_PAD + V_DIM] = 1.0
        for part in range(3):
            e[part * HEAD_PAD + hh][hh * HEAD_PAD + FOX_HEAD_DIM + part] = 1.0
            e[part * HEAD_PAD + hh][nf + hh * HEAD_PAD + FOX_HEAD_DIM + 3 + part] = -1.0
            oq[hh * HEAD_PAD + FOX_HEAD_DIM + 3 + part] = 1.0
            ok[hh * HEAD_PAD + FOX_HEAD_DIM + part] = 1.0
    row = lambda r: jnp.array(r, F32).reshape(1, nf)
    return jnp.array(e, F32).astype(BF16), row(oq), row(ok), row(ov)


def _rope_tables(seq):
    r = MLA_ROPE
    inv_freq = ROPE_THETA ** (-jnp.arange(0, r, 2, dtype=F32) / r)
    ang = jnp.arange(seq, dtype=jnp.int32).astype(F32)[:, None] * inv_freq[None, :]
    cos = jnp.cos(ang)
    sin = jnp.sin(ang)
    tail = HEAD_PAD - MLA_NOPE - MLA_ROPE
    cos_t = jnp.concatenate([jnp.ones((seq, MLA_NOPE), F32), cos, cos, jnp.zeros((seq, tail), F32)], axis=1)
    sin_t = jnp.concatenate([jnp.zeros((seq, MLA_NOPE), F32), sin, sin, jnp.zeros((seq, tail), F32)], axis=1)
    return cos_t, sin_t


def kernel(x, ffn1_norm, ffn1_w_gu, ffn1_w_down, mix_norm, w_in, q_a_norm, w_q_b, kv_a_norm, w_kv_b, pool_w, pool_scale, fox_b_f, w_out, ffn2_norm, ffn2_w_gu, ffn2_w_down, final_norm):
    batch, seq, d = x.shape
    xf = x.reshape(batch * seq, d)
    cos_t, sin_t = _rope_tables(seq)
    emat, oq, ok, ov = _gate_lane_constants()
    fin = final_norm.reshape(1, d)
    for l in range(DEPTH):
        xf = _ffn(xf, ffn1_norm[l].reshape(1, d), ffn1_w_gu[l, :, :D_FF].astype(BF16),
                  ffn1_w_gu[l, :, D_FF:].astype(BF16), ffn1_w_down[l].astype(BF16), fin, False)
        win, wq, wkv, pbd, wo, wb = _prep_mix_weights(w_in[l], w_q_b[l], w_kv_b[l], pool_w[l], w_out[l])
        fb = jnp.pad(fox_b_f[l], (0, HEAD_PAD - FOX_HEADS)).reshape(1, HEAD_PAD)
        q, k, v, yb = _mix(xf, mix_norm[l].reshape(1, d), win, q_a_norm[l].reshape(1, -1), wq,
                           kv_a_norm[l].reshape(1, -1), wkv, cos_t, sin_t, pbd,
                           pool_scale[l].reshape(1, -1), fb, emat, oq, ok, ov, batch, seq)
        o = _attention(q, k, v, batch, seq)
        xf = _outproj(xf, o, yb, wo, wb)
        xf = _ffn(xf, ffn2_norm[l].reshape(1, d), ffn2_w_gu[l, :, :D_FF].astype(BF16),
                  ffn2_w_gu[l, :, D_FF:].astype(BF16), ffn2_w_down[l].astype(BF16), fin, l == DEPTH - 1)
    return xf.reshape(batch, seq, d)
```

```python
import functools
import math

import jax
import jax.numpy as jnp
from jax import lax
from jax.experimental import pallas as pl
from jax.experimental.pallas import tpu as pltpu

F32 = jnp.float32
BF16 = jnp.bfloat16

D_MODEL = 1024
D_FF = 2816
DEPTH = 2
EPS = 1e-6
MLA_HEADS = 6
MLA_Q_RANK = 256
MLA_KV_RANK = 128
MLA_NOPE = 64
MLA_ROPE = 32
MLA_V = 64
ROPE_THETA = 10000.0
POOL_WINDOWS = (2, 4, 8, 16)
POOL_GROUP = 64
POOL_WIDTH = 256
FOX_HEADS = 6
FOX_HEAD_DIM = 64
HEADS = MLA_HEADS + FOX_HEADS
HEAD_PAD = 128
V_DIM = 64
LOG2E = math.log2(math.e)
SCALE_MLA = LOG2E / math.sqrt(MLA_NOPE + MLA_ROPE)
SCALE_FOX = LOG2E / math.sqrt(FOX_HEAD_DIM)
POOL_HISTORY = 16
NEG_BIG = -1e30

C_QA = 0
C_KVA = C_QA + MLA_Q_RANK
C_KR = C_KVA + MLA_KV_RANK
C_KRR = C_KR + HEAD_PAD
C_FF = C_KRR + HEAD_PAD
C_POOL = C_FF + HEAD_PAD
C_FQ = C_POOL + POOL_WIDTH
C_FK = C_FQ + FOX_HEADS * HEAD_PAD
C_FV = C_FK + FOX_HEADS * HEAD_PAD
N_IN_PAD = C_FV + FOX_HEADS * HEAD_PAD

VMEM_LIMIT = 56 * 1024 * 1024

FFN_TM = 512
FFN_FC = 1536
MIX_TS = 512
ATT_T = 512
ATT_RC = 64
OUT_TM = 1024


def _rms(x, g):
    return x * lax.rsqrt(jnp.mean(x * x, axis=-1, keepdims=True) + EPS) * g


def _split3(x):
    hi = x.astype(BF16)
    r = x - hi.astype(F32)
    mid = r.astype(BF16)
    lo = (r - mid.astype(F32)).astype(BF16)
    return hi, mid, lo


def _const_spec(shape):
    return pl.BlockSpec(shape, lambda *_: (0,) * len(shape))


def _ffn_kernel(x_ref, g_ref, wg_ref, wu_ref, wd_ref, fin_ref, o_ref, a_ref, *, fc, final):
    x = x_ref[...]
    h = _rms(x, g_ref[...]).astype(BF16)
    for lo in range(0, D_FF, fc):
        sl = slice(lo, min(lo + fc, D_FF))
        g = jnp.dot(h, wg_ref[:, sl], preferred_element_type=F32)
        u = jnp.dot(h, wu_ref[:, sl], preferred_element_type=F32)
        a_ref[:, sl] = (g * jax.nn.sigmoid(g) * u).astype(BF16)
    y = x + 0.5 * jnp.dot(a_ref[...], wd_ref[...], preferred_element_type=F32)
    if final:
        y = _rms(y, fin_ref[...])
    o_ref[...] = y


def _ffn(x, g, wg, wu, wd, fin, final):
    t = x.shape[0]
    tm = FFN_TM
    return pl.pallas_call(
        functools.partial(_ffn_kernel, fc=FFN_FC, final=final),
        grid=(t // tm,),
        in_specs=[
            pl.BlockSpec((tm, D_MODEL), lambda i: (i, 0)),
            _const_spec((1, D_MODEL)),
            _const_spec((D_MODEL, D_FF)),
            _const_spec((D_MODEL, D_FF)),
            _const_spec((D_FF, D_MODEL)),
            _const_spec((1, D_MODEL)),
        ],
        out_specs=pl.BlockSpec((tm, D_MODEL), lambda i: (i, 0)),
        out_shape=jax.ShapeDtypeStruct((t, D_MODEL), F32),
        scratch_shapes=[pltpu.VMEM((tm, D_FF), BF16)],
        compiler_params=pltpu.CompilerParams(
            dimension_semantics=("arbitrary",), vmem_limit_bytes=VMEM_LIMIT),
        name="ffn",
    )(x, g, wg, wu, wd, fin)


def _mix_kernel(x_ref, gn_ref, win_ref, qan_ref, wq_ref, kvn_ref, wkv_ref, cos_ref, sin_ref,
                pbd_ref, pscale_ref, fb_ref, e_ref, oq_ref, ok_ref, ov_ref,
                q_out, k_out, v_out, yb_out, hist_ref, carry_ref, *, ts):
    si = pl.program_id(1)

    @pl.when(si == 0)
    def _():
        hist_ref[...] = jnp.zeros_like(hist_ref)
        carry_ref[...] = jnp.zeros_like(carry_ref)

    h = _rms(x_ref[...], gn_ref[...]).astype(BF16)

    def proj(lo, hi):
        return jnp.dot(h, win_ref[:, lo:hi], preferred_element_type=F32)

    cos_t = cos_ref[...]
    sin_t = sin_ref[...]
    nq = MLA_HEADS * HEAD_PAD

    hq = _rms(proj(C_QA, C_KVA), qan_ref[...]).astype(BF16)
    zq = jnp.dot(hq, wq_ref[...], preferred_element_type=F32)
    for hh in range(MLA_HEADS):
        a = zq[:, hh * HEAD_PAD:(hh + 1) * HEAD_PAD]
        b = zq[:, nq + hh * HEAD_PAD:nq + (hh + 1) * HEAD_PAD]
        q_out[:, hh * HEAD_PAD:(hh + 1) * HEAD_PAD] = ((a * cos_t + b * sin_t) * SCALE_MLA).astype(BF16)

    z_kv_kr = proj(C_KVA, C_KRR)
    z_krr_ff = proj(C_KRR, C_POOL)
    hkv = _rms(z_kv_kr[:, :MLA_KV_RANK], kvn_ref[...]).astype(BF16)
    zkv = jnp.dot(hkv, wkv_ref[...], preferred_element_type=F32)
    kpe = z_kv_kr[:, MLA_KV_RANK:] * cos_t + z_krr_ff[:, :HEAD_PAD] * sin_t
    for hh in range(MLA_HEADS):
        k_out[:, hh * HEAD_PAD:(hh + 1) * HEAD_PAD] = (zkv[:, hh * HEAD_PAD:(hh + 1) * HEAD_PAD] + kpe).astype(BF16)
    ones_v = ov_ref[...]
    v_out[:, 0:nq] = (zkv[:, nq:2 * nq] + ones_v).astype(BF16)

    u = proj(C_POOL, C_FQ)
    e = jnp.concatenate([hist_ref[...], u], axis=0)
    s2 = e + pltpu.roll(e, 1, 0)
    s4 = s2 + pltpu.roll(s2, 2, 0)
    s8 = s4 + pltpu.roll(s4, 4, 0)
    s16 = s8 + pltpu.roll(s8, 8, 0)
    hist_ref[...] = u[ts - POOL_HISTORY:, :]
    lane = lax.broadcasted_iota(jnp.int32, (ts, POOL_WIDTH), 1)
    row = lax.broadcasted_iota(jnp.int32, (ts, POOL_WIDTH), 0)
    g_id = lane // POOL_GROUP
    wsum = jnp.where(g_id == 0, s2[POOL_HISTORY:], jnp.where(g_id == 1, s4[POOL_HISTORY:],
                     jnp.where(g_id == 2, s8[POOL_HISTORY:], s16[POOL_HISTORY:])))
    width = jnp.where(g_id == 0, 2, jnp.where(g_id == 1, 4, jnp.where(g_id == 2, 8, 16)))
    count = jnp.minimum(si * ts + row + 1, width).astype(F32)
    pooled = (wsum / count - u).astype(BF16)
    yb = jnp.dot(pooled, pbd_ref[...], preferred_element_type=F32) * pscale_ref[...]
    yb_out[...] = yb.astype(BF16)

    zf = z_krr_ff[:, HEAD_PAD:] + fb_ref[...]
    cum = -(jnp.maximum(-zf, 0.0) + jnp.log1p(jnp.exp(-jnp.abs(zf))))
    g_row = lax.broadcasted_iota(jnp.int32, (ts, HEAD_PAD), 0)
    shift = 1
    while shift < ts:
        cum = cum + jnp.where(g_row >= shift, pltpu.roll(cum, shift, 0), 0.0)
        shift *= 2
    cum = cum + carry_ref[...]
    carry_ref[...] = cum[ts - 1:ts, :]
    g_lane = lax.broadcasted_iota(jnp.int32, (ts, HEAD_PAD), 1)
    c0 = jnp.where(g_lane < FOX_HEADS, cum * LOG2E, 0.0)
    c_rep = c0 + pltpu.roll(c0, FOX_HEADS, 1) + pltpu.roll(c0, 2 * FOX_HEADS, 1)
    c_hi, c_mid, c_lo = _split3(c_rep)
    c_parts = jnp.where(g_lane < FOX_HEADS, c_hi, jnp.where(g_lane < 2 * FOX_HEADS, c_mid, c_lo))
    ce = jnp.dot(c_parts, e_ref[...], preferred_element_type=F32)
    nf = FOX_HEADS * HEAD_PAD
    q_out[:, nq:nq + nf] = (proj(C_FQ, C_FK) * SCALE_FOX + ce[:, :nf] + oq_ref[...]).astype(BF16)
    k_out[:, nq:nq + nf] = (proj(C_FK, C_FV) + ce[:, nf:] + ok_ref[...]).astype(BF16)
    v_out[:, nq:nq + nf] = (proj(C_FV, N_IN_PAD) + ones_v).astype(BF16)


def _mix(x, gn, win, qan, wq, kvn, wkv, cos_t, sin_t, pbd, pscale, fb, emat, oq, ok, ov, batch, seq):
    t = x.shape[0]
    ts = MIX_TS
    ns = seq // ts
    row = lambda b, s: (b * ns + s, 0)
    tab = lambda b, s: (s, 0)
    return pl.pallas_call(
        functools.partial(_mix_kernel, ts=ts),
        grid=(batch, ns),
        in_specs=[
            pl.BlockSpec((ts, D_MODEL), row),
            _const_spec((1, D_MODEL)),
            _const_spec((D_MODEL, N_IN_PAD)),
            _const_spec((1, MLA_Q_RANK)),
            _const_spec(wq.shape),
            _const_spec((1, MLA_KV_RANK)),
            _const_spec(wkv.shape),
            pl.BlockSpec((ts, HEAD_PAD), tab),
            pl.BlockSpec((ts, HEAD_PAD), tab),
            _const_spec(pbd.shape),
            _const_spec((1, POOL_WIDTH)),
            _const_spec((1, HEAD_PAD)),
            _const_spec(emat.shape),
            _const_spec(oq.shape),
            _const_spec(ok.shape),
            _const_spec(ov.shape),
        ],
        out_specs=[
            pl.BlockSpec((ts, HEADS * HEAD_PAD), row),
            pl.BlockSpec((ts, HEADS * HEAD_PAD), row),
            pl.BlockSpec((ts, HEADS * HEAD_PAD), row),
            pl.BlockSpec((ts, POOL_WIDTH), row),
        ],
        out_shape=[
            jax.ShapeDtypeStruct((t, HEADS * HEAD_PAD), BF16),
            jax.ShapeDtypeStruct((t, HEADS * HEAD_PAD), BF16),
            jax.ShapeDtypeStruct((t, HEADS * HEAD_PAD), BF16),
            jax.ShapeDtypeStruct((t, POOL_WIDTH), BF16),
        ],
        scratch_shapes=[pltpu.VMEM((POOL_HISTORY, POOL_WIDTH), F32), pltpu.VMEM((1, HEAD_PAD), F32)],
        compiler_params=pltpu.CompilerParams(
            dimension_semantics=("arbitrary", "arbitrary"), vmem_limit_bytes=VMEM_LIMIT),
        name="mixproj",
    )(x, gn, win, qan, wq, kvn, wkv, cos_t, sin_t, pbd, pscale, fb, emat, oq, ok, ov)


def _attn_kernel(q_ref, k_ref, v_ref, o_ref, sa_ref, sb_ref, pa_ref, pb_ref, m_ref, acc_ref, *, t, rc):
    i = pl.program_id(2)
    nkb = t // HEAD_PAD
    m_ref[...] = jnp.full(m_ref.shape, NEG_BIG, F32)
    acc_ref[...] = jnp.zeros(acc_ref.shape, F32)

    def scores(s_ref, j):
        for hh in range(2):
            cols = slice(hh * HEAD_PAD, (hh + 1) * HEAD_PAD)
            k = k_ref[pl.ds(j * t, t), cols]
            s_ref[hh] = lax.dot_general(q_ref[:, cols], k, (((1,), (1,)), ((), ())),
                                        preferred_element_type=F32)

    def softmax_pv(s_ref, p_ref, j, masked):
        for hh in range(2):
            cols = slice(hh * HEAD_PAD, (hh + 1) * HEAD_PAD)
            for c in range(t // rc):
                r0 = c * rc
                rows = slice(r0, r0 + rc)
                live = (r0 + rc - 1) // HEAD_PAD + 1 if masked else nkb
                blocks = []
                for kb in range(live):
                    blk = s_ref[hh, rows, kb * HEAD_PAD:(kb + 1) * HEAD_PAD]
                    if masked and (kb + 1) * HEAD_PAD - 1 > r0:
                        r_i = r0 + lax.broadcasted_iota(jnp.int32, (rc, HEAD_PAD), 0)
                        c_i = kb * HEAD_PAD + lax.broadcasted_iota(jnp.int32, (rc, HEAD_PAD), 1)
                        blk = jnp.where(c_i <= r_i, blk, NEG_BIG)
                    blocks.append(blk)
                bmax = blocks[0]
                for blk in blocks[1:]:
                    bmax = jnp.maximum(bmax, blk)
                m_old = m_ref[hh, rows, :]
                m_new = jnp.maximum(m_old, jnp.max(bmax, axis=-1, keepdims=True))
                m_ref[hh, rows, :] = m_new
                acc_ref[hh, rows, :] = jnp.exp2(m_old - m_new) * acc_ref[hh, rows, :]
                for kb in range(nkb):
                    pcols = slice(kb * HEAD_PAD, (kb + 1) * HEAD_PAD)
                    if kb < live:
                        p_ref[hh, rows, pcols] = jnp.exp2(blocks[kb] - m_new).astype(BF16)
                    else:
                        p_ref[hh, rows, pcols] = jnp.zeros((rc, HEAD_PAD), BF16)
            v = v_ref[pl.ds(j * t, t), cols]
            acc_ref[hh] += jnp.dot(p_ref[hh], v, preferred_element_type=F32)

    scores(sa_ref, 0)

    def body(jj, carry):
        j = 2 * jj
        scores(sb_ref, j + 1)
        softmax_pv(sa_ref, pa_ref, j, False)
        scores(sa_ref, j + 2)
        softmax_pv(sb_ref, pb_ref, j + 1, False)
        return carry

    lax.fori_loop(0, i // 2, body, 0)

    @pl.when(i % 2 == 0)
    def _():
        softmax_pv(sa_ref, pa_ref, i, True)

    @pl.when(i % 2 == 1)
    def _():
        scores(sb_ref, i)
        softmax_pv(sa_ref, pa_ref, i - 1, False)
        softmax_pv(sb_ref, pb_ref, i, True)

    outs = []
    for hh in range(2):
        acc = acc_ref[hh]
        outs.append(acc[:, :V_DIM] / acc[:, V_DIM:V_DIM + 1])
    o_ref[...] = jnp.concatenate(outs, axis=1).astype(BF16)


def _attention(q, k, v, batch, seq):
    t = ATT_T
    nq = seq // t
    pairs = HEADS // 2
    return pl.pallas_call(
        functools.partial(_attn_kernel, t=t, rc=ATT_RC),
        grid=(batch, pairs, nq),
        in_specs=[
            pl.BlockSpec((t, 2 * HEAD_PAD), lambda b, p, i: (b * nq + i, p)),
            pl.BlockSpec((seq, 2 * HEAD_PAD), lambda b, p, i: (b, p)),
            pl.BlockSpec((seq, 2 * HEAD_PAD), lambda b, p, i: (b, p)),
        ],
        out_specs=pl.BlockSpec((t, 2 * V_DIM), lambda b, p, i: (b * nq + i, p)),
        out_shape=jax.ShapeDtypeStruct((batch * seq, HEADS * V_DIM), BF16),
        scratch_shapes=[
            pltpu.VMEM((2, t, t), F32),
            pltpu.VMEM((2, t, t), F32),
            pltpu.VMEM((2, t, t), BF16),
            pltpu.VMEM((2, t, t), BF16),
            pltpu.VMEM((2, t, HEAD_PAD), F32),
            pltpu.VMEM((2, t, HEAD_PAD), F32),
        ],
        compiler_params=pltpu.CompilerParams(
            dimension_semantics=("arbitrary", "arbitrary", "arbitrary"), vmem_limit_bytes=VMEM_LIMIT),
        name="attention",
    )(q, k, v)


def _out_kernel(x_ref, o_ref, yb_ref, wo_ref, wb_ref, out_ref):
    out_ref[...] = (x_ref[...]
                    + jnp.dot(o_ref[...], wo_ref[...], preferred_element_type=F32)
                    + jnp.dot(yb_ref[...], wb_ref[...], preferred_element_type=F32))


def _outproj(x, o, yb, wo, wb):
    t = x.shape[0]
    tm = OUT_TM
    return pl.pallas_call(
        _out_kernel,
        grid=(t // tm,),
        in_specs=[
            pl.BlockSpec((tm, D_MODEL), lambda i: (i, 0)),
            pl.BlockSpec((tm, HEADS * V_DIM), lambda i: (i, 0)),
            pl.BlockSpec((tm, POOL_WIDTH), lambda i: (i, 0)),
            _const_spec(wo.shape),
            _const_spec(wb.shape),
        ],
        out_specs=pl.BlockSpec((tm, D_MODEL), lambda i: (i, 0)),
        out_shape=jax.ShapeDtypeStruct((t, D_MODEL), F32),
        compiler_params=pltpu.CompilerParams(
            dimension_semantics=("arbitrary",), vmem_limit_bytes=VMEM_LIMIT),
        name="outproj",
    )(x, o, yb, wo, wb)


def _head_blocks(w, n_heads, width):
    k = w.shape[0]
    w = w.reshape(k, n_heads, width)
    return jnp.pad(w, ((0, 0), (0, 0), (0, HEAD_PAD - width))).reshape(k, n_heads * HEAD_PAD)


def _rot_cols(w):
    half = w.shape[-1] // 2
    return jnp.concatenate([-w[..., half:], w[..., :half]], axis=-1)


def _prep_mix_weights(w_in, w_q_b, w_kv_b, pool_w, w_out):
    o1 = MLA_Q_RANK
    o2 = o1 + MLA_KV_RANK
    o3 = o2 + MLA_ROPE
    o4 = o3 + POOL_WIDTH
    o5 = o4 + 3 * FOX_HEADS * FOX_HEAD_DIM
    hd = FOX_HEADS * FOX_HEAD_DIM
    d = w_in.shape[0]
    w_kr = w_in[:, o2:o3]
    pad_rope = lambda w: jnp.pad(w, ((0, 0), (MLA_NOPE, HEAD_PAD - MLA_NOPE - MLA_ROPE)))
    win = jnp.concatenate([
        w_in[:, 0:o2],
        pad_rope(w_kr),
        pad_rope(_rot_cols(w_kr)),
        jnp.pad(w_in[:, o5:], ((0, 0), (0, HEAD_PAD - FOX_HEADS))),
        w_in[:, o3:o4],
        _head_blocks(w_in[:, o4:o4 + hd], FOX_HEADS, FOX_HEAD_DIM),
        _head_blocks(w_in[:, o4 + hd:o4 + 2 * hd], FOX_HEADS, FOX_HEAD_DIM),
        _head_blocks(w_in[:, o4 + 2 * hd:o5], FOX_HEADS, FOX_HEAD_DIM),
    ], axis=1).astype(BF16)
    assert win.shape == (d, N_IN_PAD)

    wq3 = w_q_b.reshape(MLA_Q_RANK, MLA_HEADS, MLA_NOPE + MLA_ROPE)
    plain = jnp.pad(wq3, ((0, 0), (0, 0), (0, HEAD_PAD - MLA_NOPE - MLA_ROPE)))
    rot = jnp.pad(_rot_cols(wq3[..., MLA_NOPE:]), ((0, 0), (0, 0), (MLA_NOPE, HEAD_PAD - MLA_NOPE - MLA_ROPE)))
    wq = jnp.concatenate([plain.reshape(MLA_Q_RANK, -1), rot.reshape(MLA_Q_RANK, -1)], axis=1).astype(BF16)

    wkv3 = w_kv_b.reshape(MLA_KV_RANK, MLA_HEADS, MLA_NOPE + MLA_V)
    wk = jnp.pad(wkv3[..., :MLA_NOPE], ((0, 0), (0, 0), (0, HEAD_PAD - MLA_NOPE))).reshape(MLA_KV_RANK, -1)
    wv = jnp.pad(wkv3[..., MLA_NOPE:], ((0, 0), (0, 0), (0, HEAD_PAD - MLA_V))).reshape(MLA_KV_RANK, -1)
    wkv = jnp.concatenate([wk, wv], axis=1).astype(BF16)

    pbd = jnp.zeros((POOL_WIDTH, POOL_WIDTH), F32)
    for g in range(len(POOL_WINDOWS)):
        pbd = pbd.at[g * POOL_GROUP:(g + 1) * POOL_GROUP, g * POOL_GROUP:(g + 1) * POOL_GROUP].set(pool_w[g])
    pbd = pbd.astype(BF16)

    na = MLA_HEADS * MLA_V
    wo = jnp.concatenate([w_out[:na], w_out[na + POOL_WIDTH:]], axis=0).astype(BF16)
    wb = w_out[na:na + POOL_WIDTH].astype(BF16)
    return win, wq, wkv, pbd, wo, wb


def _gate_lane_constants():
    nf = FOX_HEADS * HEAD_PAD
    e = [[0.0] * (2 * nf) for _ in range(HEAD_PAD)]
    oq = [0.0] * nf
    ok = [0.0] * nf
    ov = [0.0] * nf
    for hh in range(FOX_HEADS):
        ov[hh * HEAD_PAD + V_DIM] = 1.0
        for part in range(3):
            e[part * FOX_HEADS + hh][hh * HEAD_PAD + FOX_HEAD_DIM + part] = 1.0
            e[part * FOX_HEADS + hh][nf + hh * HEAD_PAD + FOX_HEAD_DIM + 3 + part] = -1.0
            oq[hh * HEAD_PAD + FOX_HEAD_DIM + 3 + part] = 1.0
            ok[hh * HEAD_PAD + FOX_HEAD_DIM + part] = 1.0
    row = lambda r: jnp.array(r, F32).reshape(1, nf)
    return jnp.array(e, F32).astype(BF16), row(oq), row(ok), row(ov)


def _rope_tables(seq):
    r = MLA_ROPE
    inv_freq = ROPE_THETA ** (-jnp.arange(0, r, 2, dtype=F32) / r)
    ang = jnp.arange(seq, dtype=jnp.int32).astype(F32)[:, None] * inv_freq[None, :]
    cos = jnp.cos(ang)
    sin = jnp.sin(ang)
    tail = HEAD_PAD - MLA_NOPE - MLA_ROPE
    cos_t = jnp.concatenate([jnp.ones((seq, MLA_NOPE), F32), cos, cos, jnp.zeros((seq, tail), F32)], axis=1)
    sin_t = jnp.concatenate([jnp.zeros((seq, MLA_NOPE), F32), sin, sin, jnp.zeros((seq, tail), F32)], axis=1)
    return cos_t, sin_t


def kernel(x, ffn1_norm, ffn1_w_gu, ffn1_w_down, mix_norm, w_in, q_a_norm, w_q_b, kv_a_norm, w_kv_b, pool_w, pool_scale, fox_b_f, w_out, ffn2_norm, ffn2_w_gu, ffn2_w_down, final_norm):
    batch, seq, d = x.shape
    xf = x.reshape(batch * seq, d)
    cos_t, sin_t = _rope_tables(seq)
    emat, oq, ok, ov = _gate_lane_constants()
    fin = final_norm.reshape(1, d)
    for l in range(DEPTH):
        xf = _ffn(xf, ffn1_norm[l].reshape(1, d), ffn1_w_gu[l, :, :D_FF].astype(BF16),
                  ffn1_w_gu[l, :, D_FF:].astype(BF16), ffn1_w_down[l].astype(BF16), fin, False)
        win, wq, wkv, pbd, wo, wb = _prep_mix_weights(w_in[l], w_q_b[l], w_kv_b[l], pool_w[l], w_out[l])
        fb = jnp.pad(fox_b_f[l], (0, HEAD_PAD - FOX_HEADS)).reshape(1, HEAD_PAD)
        q, k, v, yb = _mix(xf, mix_norm[l].reshape(1, d), win, q_a_norm[l].reshape(1, -1), wq,
                           kv_a_norm[l].reshape(1, -1), wkv, cos_t, sin_t, pbd,
                           pool_scale[l].reshape(1, -1), fb, emat, oq, ok, ov, batch, seq)
        o = _attention(q, k, v, batch, seq)
        xf = _outproj(xf, o, yb, wo, wb)
        xf = _ffn(xf, ffn2_norm[l].reshape(1, d), ffn2_w_gu[l, :, :D_FF].astype(BF16),
                  ffn2_w_gu[l, :, D_FF:].astype(BF16), ffn2_w_down[l].astype(BF16), fin, l == DEPTH - 1)
    return xf.reshape(batch, seq, d)
```

```python
import functools
import math

import jax
import jax.numpy as jnp
from jax import lax
from jax.experimental import pallas as pl
from jax.experimental.pallas import tpu as pltpu

F32 = jnp.float32
BF16 = jnp.bfloat16

D_MODEL = 1024
D_FF = 2816
DEPTH = 2
EPS = 1e-6
MLA_HEADS = 6
MLA_Q_RANK = 256
MLA_KV_RANK = 128
MLA_NOPE = 64
MLA_ROPE = 32
MLA_V = 64
ROPE_THETA = 10000.0
POOL_WINDOWS = (2, 4, 8, 16)
POOL_GROUP = 64
POOL_WIDTH = 256
FOX_HEADS = 6
FOX_HEAD_DIM = 64
HEADS = MLA_HEADS + FOX_HEADS
HEAD_PAD = 128
V_DIM = 64
LOG2E = math.log2(math.e)
SCALE_MLA = LOG2E / math.sqrt(MLA_NOPE + MLA_ROPE)
SCALE_FOX = LOG2E / math.sqrt(FOX_HEAD_DIM)
POOL_HISTORY = 16
NEG_BIG = -1e30

C_QA = 0
C_KVA = C_QA + MLA_Q_RANK
C_KR = C_KVA + MLA_KV_RANK
C_KRR = C_KR + HEAD_PAD
C_FF = C_KRR + HEAD_PAD
C_POOL = C_FF + HEAD_PAD
C_FQ = C_POOL + POOL_WIDTH
C_FK = C_FQ + FOX_HEADS * HEAD_PAD
C_FV = C_FK + FOX_HEADS * HEAD_PAD
N_IN_PAD = C_FV + FOX_HEADS * HEAD_PAD

VMEM_LIMIT = 56 * 1024 * 1024

FFN_TM = 512
FFN_FC = 1536
MIX_TS = 512
ATT_T = 512
ATT_RC = 64
OUT_TM = 1024


def _rms(x, g):
    return x * lax.rsqrt(jnp.mean(x * x, axis=-1, keepdims=True) + EPS) * g


def _split3(x):
    hi = x.astype(BF16)
    r = x - hi.astype(F32)
    mid = r.astype(BF16)
    lo = (r - mid.astype(F32)).astype(BF16)
    return hi, mid, lo


def _const_spec(shape):
    return pl.BlockSpec(shape, lambda *_: (0,) * len(shape))


def _ffn_kernel(x_ref, g_ref, wg_ref, wu_ref, wd_ref, fin_ref, o_ref, a_ref, *, fc, final):
    x = x_ref[...]
    h = _rms(x, g_ref[...]).astype(BF16)
    for lo in range(0, D_FF, fc):
        sl = slice(lo, min(lo + fc, D_FF))
        g = jnp.dot(h, wg_ref[:, sl], preferred_element_type=F32)
        u = jnp.dot(h, wu_ref[:, sl], preferred_element_type=F32)
        a_ref[:, sl] = (g * jax.nn.sigmoid(g) * u).astype(BF16)
    y = x + 0.5 * jnp.dot(a_ref[...], wd_ref[...], preferred_element_type=F32)
    if final:
        y = _rms(y, fin_ref[...])
    o_ref[...] = y


def _ffn(x, g, wg, wu, wd, fin, final):
    t = x.shape[0]
    tm = FFN_TM
    return pl.pallas_call(
        functools.partial(_ffn_kernel, fc=FFN_FC, final=final),
        grid=(t // tm,),
        in_specs=[
            pl.BlockSpec((tm, D_MODEL), lambda i: (i, 0)),
            _const_spec((1, D_MODEL)),
            _const_spec((D_MODEL, D_FF)),
            _const_spec((D_MODEL, D_FF)),
            _const_spec((D_FF, D_MODEL)),
            _const_spec((1, D_MODEL)),
        ],
        out_specs=pl.BlockSpec((tm, D_MODEL), lambda i: (i, 0)),
        out_shape=jax.ShapeDtypeStruct((t, D_MODEL), F32),
        scratch_shapes=[pltpu.VMEM((tm, D_FF), BF16)],
        compiler_params=pltpu.CompilerParams(
            dimension_semantics=("arbitrary",), vmem_limit_bytes=VMEM_LIMIT),
        name="ffn",
    )(x, g, wg, wu, wd, fin)


def _mix_kernel(x_ref, gn_ref, win_ref, qan_ref, wq_ref, kvn_ref, wkv_ref, cos_ref, sin_ref,
                pbd_ref, pscale_ref, fb_ref, e_ref, oq_ref, ok_ref, ov_ref,
                q_out, k_out, v_out, yb_out, hist_ref, carry_ref, *, ts):
    si = pl.program_id(1)

    @pl.when(si == 0)
    def _():
        hist_ref[...] = jnp.zeros_like(hist_ref)
        carry_ref[...] = jnp.zeros_like(carry_ref)

    h = _rms(x_ref[...], gn_ref[...]).astype(BF16)

    def proj(lo, hi):
        return jnp.dot(h, win_ref[:, lo:hi], preferred_element_type=F32)

    cos_t = cos_ref[...]
    sin_t = sin_ref[...]
    nq = MLA_HEADS * HEAD_PAD

    hq = _rms(proj(C_QA, C_KVA), qan_ref[...]).astype(BF16)
    zq = jnp.dot(hq, wq_ref[...], preferred_element_type=F32)
    for hh in range(MLA_HEADS):
        a = zq[:, hh * HEAD_PAD:(hh + 1) * HEAD_PAD]
        b = zq[:, nq + hh * HEAD_PAD:nq + (hh + 1) * HEAD_PAD]
        q_out[:, hh * HEAD_PAD:(hh + 1) * HEAD_PAD] = ((a * cos_t + b * sin_t) * SCALE_MLA).astype(BF16)

    z_kv_kr = proj(C_KVA, C_KRR)
    z_krr_ff = proj(C_KRR, C_POOL)
    hkv = _rms(z_kv_kr[:, :MLA_KV_RANK], kvn_ref[...]).astype(BF16)
    zkv = jnp.dot(hkv, wkv_ref[...], preferred_element_type=F32)
    kpe = z_kv_kr[:, MLA_KV_RANK:] * cos_t + z_krr_ff[:, :HEAD_PAD] * sin_t
    for hh in range(MLA_HEADS):
        k_out[:, hh * HEAD_PAD:(hh + 1) * HEAD_PAD] = (zkv[:, hh * HEAD_PAD:(hh + 1) * HEAD_PAD] + kpe).astype(BF16)
    ones_v = ov_ref[...]
    v_out[:, 0:nq] = (zkv[:, nq:2 * nq] + ones_v).astype(BF16)

    u = proj(C_POOL, C_FQ)
    e = jnp.concatenate([hist_ref[...], u], axis=0)
    s2 = e + pltpu.roll(e, 1, 0)
    s4 = s2 + pltpu.roll(s2, 2, 0)
    s8 = s4 + pltpu.roll(s4, 4, 0)
    s16 = s8 + pltpu.roll(s8, 8, 0)
    hist_ref[...] = u[ts - POOL_HISTORY:, :]
    lane = lax.broadcasted_iota(jnp.int32, (ts, POOL_WIDTH), 1)
    row = lax.broadcasted_iota(jnp.int32, (ts, POOL_WIDTH), 0)
    g_id = lane // POOL_GROUP
    wsum = jnp.where(g_id == 0, s2[POOL_HISTORY:], jnp.where(g_id == 1, s4[POOL_HISTORY:],
                     jnp.where(g_id == 2, s8[POOL_HISTORY:], s16[POOL_HISTORY:])))
    width = jnp.where(g_id == 0, 2, jnp.where(g_id == 1, 4, jnp.where(g_id == 2, 8, 16)))
    count = jnp.minimum(si * ts + row + 1, width).astype(F32)
    pooled = (wsum / count - u).astype(BF16)
    yb = jnp.dot(pooled, pbd_ref[...], preferred_element_type=F32) * pscale_ref[...]
    yb_out[...] = yb.astype(BF16)

    zf = z_krr_ff[:, HEAD_PAD:] + fb_ref[...]
    cum = -(jnp.maximum(-zf, 0.0) + jnp.log1p(jnp.exp(-jnp.abs(zf))))
    g_row = lax.broadcasted_iota(jnp.int32, (ts, HEAD_PAD), 0)
    shift = 1
    while shift < ts:
        cum = cum + jnp.where(g_row >= shift, pltpu.roll(cum, shift, 0), 0.0)
        shift *= 2
    cum = cum + carry_ref[...]
    carry_ref[...] = cum[ts - 1:ts, :]
    g_lane = lax.broadcasted_iota(jnp.int32, (ts, HEAD_PAD), 1)
    c0 = jnp.where(g_lane < FOX_HEADS, cum * LOG2E, 0.0)
    c_rep = c0 + pltpu.roll(c0, FOX_HEADS, 1) + pltpu.roll(c0, 2 * FOX_HEADS, 1)
    c_hi, c_mid, c_lo = _split3(c_rep)
    c_parts = jnp.where(g_lane < FOX_HEADS, c_hi, jnp.where(g_lane < 2 * FOX_HEADS, c_mid, c_lo))
    ce = jnp.dot(c_parts, e_ref[...], preferred_element_type=F32)
    nf = FOX_HEADS * HEAD_PAD
    q_out[:, nq:nq + nf] = (proj(C_FQ, C_FK) * SCALE_FOX + ce[:, :nf] + oq_ref[...]).astype(BF16)
    k_out[:, nq:nq + nf] = (proj(C_FK, C_FV) + ce[:, nf:] + ok_ref[...]).astype(BF16)
    v_out[:, nq:nq + nf] = (proj(C_FV, N_IN_PAD) + ones_v).astype(BF16)


def _mix(x, gn, win, qan, wq, kvn, wkv, cos_t, sin_t, pbd, pscale, fb, emat, oq, ok, ov, batch, seq):
    t = x.shape[0]
    ts = MIX_TS
    ns = seq // ts
    row = lambda b, s: (b * ns + s, 0)
    tab = lambda b, s: (s, 0)
    return pl.pallas_call(
        functools.partial(_mix_kernel, ts=ts),
        grid=(batch, ns),
        in_specs=[
            pl.BlockSpec((ts, D_MODEL), row),
            _const_spec((1, D_MODEL)),
            _const_spec((D_MODEL, N_IN_PAD)),
            _const_spec((1, MLA_Q_RANK)),
            _const_spec(wq.shape),
            _const_spec((1, MLA_KV_RANK)),
            _const_spec(wkv.shape),
            pl.BlockSpec((ts, HEAD_PAD), tab),
            pl.BlockSpec((ts, HEAD_PAD), tab),
            _const_spec(pbd.shape),
            _const_spec((1, POOL_WIDTH)),
            _const_spec((1, HEAD_PAD)),
            _const_spec(emat.shape),
            _const_spec(oq.shape),
            _const_spec(ok.shape),
            _const_spec(ov.shape),
        ],
        out_specs=[
            pl.BlockSpec((ts, HEADS * HEAD_PAD), row),
            pl.BlockSpec((ts, HEADS * HEAD_PAD), row),
            pl.BlockSpec((ts, HEADS * HEAD_PAD), row),
            pl.BlockSpec((ts, POOL_WIDTH), row),
        ],
        out_shape=[
            jax.ShapeDtypeStruct((t, HEADS * HEAD_PAD), BF16),
            jax.ShapeDtypeStruct((t, HEADS * HEAD_PAD), BF16),
            jax.ShapeDtypeStruct((t, HEADS * HEAD_PAD), BF16),
            jax.ShapeDtypeStruct((t, POOL_WIDTH), BF16),
        ],
        scratch_shapes=[pltpu.VMEM((POOL_HISTORY, POOL_WIDTH), F32), pltpu.VMEM((1, HEAD_PAD), F32)],
        compiler_params=pltpu.CompilerParams(
            dimension_semantics=("arbitrary", "arbitrary"), vmem_limit_bytes=VMEM_LIMIT),
        name="mixproj",
    )(x, gn, win, qan, wq, kvn, wkv, cos_t, sin_t, pbd, pscale, fb, emat, oq, ok, ov)


def _attn_kernel(q_ref, k_ref, v_ref, o_ref, s_ref, p_ref, m_ref, al_ref, acc_ref, *, t, rc, nq):
    nkb = t // HEAD_PAD
    head_cols = [slice(hh * HEAD_PAD, (hh + 1) * HEAD_PAD) for hh in range(2)]

    def rows_of(blk):
        return pl.ds(pl.multiple_of(blk * t, t), t)

    def scores(i, j, hh):
        s_ref[hh] = lax.dot_general(q_ref[rows_of(i), head_cols[hh]], k_ref[rows_of(j), head_cols[hh]],
                                    (((1,), (1,)), ((), ())), preferred_element_type=F32)

    def softmax(hh, masked, first):
        for c in range(t // rc):
            r0 = c * rc
            rows = slice(r0, r0 + rc)
            live = (r0 + rc - 1) // HEAD_PAD + 1 if masked else nkb
            blocks = []
            for kb in range(live):
                blk = s_ref[hh, rows, kb * HEAD_PAD:(kb + 1) * HEAD_PAD]
                if masked and (kb + 1) * HEAD_PAD - 1 > r0:
                    r_i = r0 + lax.broadcasted_iota(jnp.int32, (rc, HEAD_PAD), 0)
                    c_i = kb * HEAD_PAD + lax.broadcasted_iota(jnp.int32, (rc, HEAD_PAD), 1)
                    blk = jnp.where(c_i <= r_i, blk, NEG_BIG)
                blocks.append(blk)
            bmax = blocks[0]
            for blk in blocks[1:]:
                bmax = jnp.maximum(bmax, blk)
            rmax = jnp.max(bmax, axis=-1, keepdims=True)
            m_old = jnp.full((rc, HEAD_PAD), NEG_BIG, F32) if first else m_ref[hh, rows, :]
            m_new = jnp.maximum(m_old, rmax)
            al_ref[hh, rows, :] = jnp.exp2(m_old - m_new)
            m_ref[hh, rows, :] = m_new
            for kb in range(nkb):
                pcols = slice(kb * HEAD_PAD, (kb + 1) * HEAD_PAD)
                if kb < live:
                    p_ref[hh, rows, pcols] = jnp.exp2(blocks[kb] - m_new).astype(BF16)
                else:
                    p_ref[hh, rows, pcols] = (blocks[0] * 0.0).astype(BF16)

    def accumulate(j, hh):
        pv = jnp.dot(p_ref[hh], v_ref[rows_of(j), head_cols[hh]], preferred_element_type=F32)
        acc_ref[hh] = al_ref[hh] * acc_ref[hh] + pv

    def finish(i):
        outs = []
        for hh in range(2):
            acc = acc_ref[hh]
            outs.append(acc[:, :V_DIM] / acc[:, V_DIM:V_DIM + 1])
        o_ref[rows_of(i), :] = jnp.concatenate(outs, axis=1).astype(BF16)

    for hh in range(2):
        zero = v_ref[0:t, head_cols[hh]].astype(F32) * 0.0
        acc_ref[hh] = zero
        if hh == 1:
            m_ref[hh] = zero + NEG_BIG
    scores(0, 0, 0)
    scores(0, 0, 1)
    softmax(0, True, True)

    def stage(i, j, ni, nj, cur_diag, next_diag):
        accumulate(j, 0)
        scores(ni, nj, 0)
        softmax(1, cur_diag, False)
        scores(ni, nj, 1)
        softmax(0, next_diag, cur_diag)
        accumulate(j, 1)
        if cur_diag:
            finish(i)
            m_ref[1] = m_ref[1] * 0.0 + NEG_BIG

    def body(n, carry):
        i, j = carry
        cur_diag = j == i
        ni = jnp.where(cur_diag, i + 1, i)
        nj = jnp.where(cur_diag, 0, j + 1)

        @pl.when(cur_diag)
        def _():
            stage(i, j, ni, nj, True, False)

        @pl.when(j + 1 == i)
        def _():
            stage(i, j, ni, nj, False, True)

        @pl.when(j + 1 < i)
        def _():
            stage(i, j, ni, nj, False, False)

        return ni, nj

    n_steps = nq * (nq + 1) // 2
    last_i, last_j = lax.fori_loop(0, n_steps - 1, body, (jnp.int32(0), jnp.int32(0)))
    softmax(1, True, False)
    accumulate(last_j, 0)
    accumulate(last_j, 1)
    finish(last_i)


def _attention(q, k, v, batch, seq):
    t = ATT_T
    nq = seq // t
    pairs = HEADS // 2
    whole = lambda b, p: (b, p)
    return pl.pallas_call(
        functools.partial(_attn_kernel, t=t, rc=ATT_RC, nq=nq),
        grid=(batch, pairs),
        in_specs=[
            pl.BlockSpec((seq, 2 * HEAD_PAD), whole),
            pl.BlockSpec((seq, 2 * HEAD_PAD), whole),
            pl.BlockSpec((seq, 2 * HEAD_PAD), whole),
        ],
        out_specs=pl.BlockSpec((seq, 2 * V_DIM), whole),
        out_shape=jax.ShapeDtypeStruct((batch * seq, HEADS * V_DIM), BF16),
        scratch_shapes=[
            pltpu.VMEM((2, t, t), F32),
            pltpu.VMEM((2, t, t), BF16),
            pltpu.VMEM((2, t, HEAD_PAD), F32),
            pltpu.VMEM((2, t, HEAD_PAD), F32),
            pltpu.VMEM((2, t, HEAD_PAD), F32),
        ],
        compiler_params=pltpu.CompilerParams(
            dimension_semantics=("arbitrary", "arbitrary"), vmem_limit_bytes=VMEM_LIMIT),
        name="attention",
    )(q, k, v)


def _out_kernel(x_ref, o_ref, yb_ref, wo_ref, wb_ref, out_ref):
    out_ref[...] = (x_ref[...]
                    + jnp.dot(o_ref[...], wo_ref[...], preferred_element_type=F32)
                    + jnp.dot(yb_ref[...], wb_ref[...], preferred_element_type=F32))


def _outproj(x, o, yb, wo, wb):
    t = x.shape[0]
    tm = OUT_TM
    return pl.pallas_call(
        _out_kernel,
        grid=(t // tm,),
        in_specs=[
            pl.BlockSpec((tm, D_MODEL), lambda i: (i, 0)),
            pl.BlockSpec((tm, HEADS * V_DIM), lambda i: (i, 0)),
            pl.BlockSpec((tm, POOL_WIDTH), lambda i: (i, 0)),
            _const_spec(wo.shape),
            _const_spec(wb.shape),
        ],
        out_specs=pl.BlockSpec((tm, D_MODEL), lambda i: (i, 0)),
        out_shape=jax.ShapeDtypeStruct((t, D_MODEL), F32),
        compiler_params=pltpu.CompilerParams(
            dimension_semantics=("arbitrary",), vmem_limit_bytes=VMEM_LIMIT),
        name="outproj",
    )(x, o, yb, wo, wb)


def _head_blocks(w, n_heads, width):
    k = w.shape[0]
    w = w.reshape(k, n_heads, width)
    return jnp.pad(w, ((0, 0), (0, 0), (0, HEAD_PAD - width))).reshape(k, n_heads * HEAD_PAD)


def _rot_cols(w):
    half = w.shape[-1] // 2
    return jnp.concatenate([-w[..., half:], w[..., :half]], axis=-1)


def _prep_mix_weights(w_in, w_q_b, w_kv_b, pool_w, w_out):
    o1 = MLA_Q_RANK
    o2 = o1 + MLA_KV_RANK
    o3 = o2 + MLA_ROPE
    o4 = o3 + POOL_WIDTH
    o5 = o4 + 3 * FOX_HEADS * FOX_HEAD_DIM
    hd = FOX_HEADS * FOX_HEAD_DIM
    d = w_in.shape[0]
    w_kr = w_in[:, o2:o3]
    pad_rope = lambda w: jnp.pad(w, ((0, 0), (MLA_NOPE, HEAD_PAD - MLA_NOPE - MLA_ROPE)))
    win = jnp.concatenate([
        w_in[:, 0:o2],
        pad_rope(w_kr),
        pad_rope(_rot_cols(w_kr)),
        jnp.pad(w_in[:, o5:], ((0, 0), (0, HEAD_PAD - FOX_HEADS))),
        w_in[:, o3:o4],
        _head_blocks(w_in[:, o4:o4 + hd], FOX_HEADS, FOX_HEAD_DIM),
        _head_blocks(w_in[:, o4 + hd:o4 + 2 * hd], FOX_HEADS, FOX_HEAD_DIM),
        _head_blocks(w_in[:, o4 + 2 * hd:o5], FOX_HEADS, FOX_HEAD_DIM),
    ], axis=1).astype(BF16)
    assert win.shape == (d, N_IN_PAD)

    wq3 = w_q_b.reshape(MLA_Q_RANK, MLA_HEADS, MLA_NOPE + MLA_ROPE)
    plain = jnp.pad(wq3, ((0, 0), (0, 0), (0, HEAD_PAD - MLA_NOPE - MLA_ROPE)))
    rot = jnp.pad(_rot_cols(wq3[..., MLA_NOPE:]), ((0, 0), (0, 0), (MLA_NOPE, HEAD_PAD - MLA_NOPE - MLA_ROPE)))
    wq = jnp.concatenate([plain.reshape(MLA_Q_RANK, -1), rot.reshape(MLA_Q_RANK, -1)], axis=1).astype(BF16)

    wkv3 = w_kv_b.reshape(MLA_KV_RANK, MLA_HEADS, MLA_NOPE + MLA_V)
    wk = jnp.pad(wkv3[..., :MLA_NOPE], ((0, 0), (0, 0), (0, HEAD_PAD - MLA_NOPE))).reshape(MLA_KV_RANK, -1)
    wv = jnp.pad(wkv3[..., MLA_NOPE:], ((0, 0), (0, 0), (0, HEAD_PAD - MLA_V))).reshape(MLA_KV_RANK, -1)
    wkv = jnp.concatenate([wk, wv], axis=1).astype(BF16)

    pbd = jnp.zeros((POOL_WIDTH, POOL_WIDTH), F32)
    for g in range(len(POOL_WINDOWS)):
        pbd = pbd.at[g * POOL_GROUP:(g + 1) * POOL_GROUP, g * POOL_GROUP:(g + 1) * POOL_GROUP].set(pool_w[g])
    pbd = pbd.astype(BF16)

    na = MLA_HEADS * MLA_V
    wo = jnp.concatenate([w_out[:na], w_out[na + POOL_WIDTH:]], axis=0).astype(BF16)
    wb = w_out[na:na + POOL_WIDTH].astype(BF16)
    return win, wq, wkv, pbd, wo, wb


def _gate_lane_constants():
    nf = FOX_HEADS * HEAD_PAD
    e = [[0.0] * (2 * nf) for _ in range(HEAD_PAD)]
    oq = [0.0] * nf
    ok = [0.0] * nf
    ov = [0.0] * nf
    for hh in range(FOX_HEADS):
        ov[hh * HEAD_PAD + V_DIM] = 1.0
        for part in range(3):
            e[part * FOX_HEADS + hh][hh * HEAD_PAD + FOX_HEAD_DIM + part] = 1.0
            e[part * FOX_HEADS + hh][nf + hh * HEAD_PAD + FOX_HEAD_DIM + 3 + part] = -1.0
            oq[hh * HEAD_PAD + FOX_HEAD_DIM + 3 + part] = 1.0
            ok[hh * HEAD_PAD + FOX_HEAD_DIM + part] = 1.0
    row = lambda r: jnp.array(r, F32).reshape(1, nf)
    return jnp.array(e, F32).astype(BF16), row(oq), row(ok), row(ov)


def _rope_tables(seq):
    r = MLA_ROPE
    inv_freq = ROPE_THETA ** (-jnp.arange(0, r, 2, dtype=F32) / r)
    ang = jnp.arange(seq, dtype=jnp.int32).astype(F32)[:, None] * inv_freq[None, :]
    cos = jnp.cos(ang)
    sin = jnp.sin(ang)
    tail = HEAD_PAD - MLA_NOPE - MLA_ROPE
    cos_t = jnp.concatenate([jnp.ones((seq, MLA_NOPE), F32), cos, cos, jnp.zeros((seq, tail), F32)], axis=1)
    sin_t = jnp.concatenate([jnp.zeros((seq, MLA_NOPE), F32), sin, sin, jnp.zeros((seq, tail), F32)], axis=1)
    return cos_t, sin_t


def kernel(x, ffn1_norm, ffn1_w_gu, ffn1_w_down, mix_norm, w_in, q_a_norm, w_q_b, kv_a_norm, w_kv_b, pool_w, pool_scale, fox_b_f, w_out, ffn2_norm, ffn2_w_gu, ffn2_w_down, final_norm):
    batch, seq, d = x.shape
    xf = x.reshape(batch * seq, d)
    cos_t, sin_t = _rope_tables(seq)
    emat, oq, ok, ov = _gate_lane_constants()
    fin = final_norm.reshape(1, d)
    for l in range(DEPTH):
        xf = _ffn(xf, ffn1_norm[l].reshape(1, d), ffn1_w_gu[l, :, :D_FF].astype(BF16),
                  ffn1_w_gu[l, :, D_FF:].astype(BF16), ffn1_w_down[l].astype(BF16), fin, False)
        win, wq, wkv, pbd, wo, wb = _prep_mix_weights(w_in[l], w_q_b[l], w_kv_b[l], pool_w[l], w_out[l])
        fb = jnp.pad(fox_b_f[l], (0, HEAD_PAD - FOX_HEADS)).reshape(1, HEAD_PAD)
        q, k, v, yb = _mix(xf, mix_norm[l].reshape(1, d), win, q_a_norm[l].reshape(1, -1), wq,
                           kv_a_norm[l].reshape(1, -1), wkv, cos_t, sin_t, pbd,
                           pool_scale[l].reshape(1, -1), fb, emat, oq, ok, ov, batch, seq)
        o = _attention(q, k, v, batch, seq)
        xf = _outproj(xf, o, yb, wo, wb)
        xf = _ffn(xf, ffn2_norm[l].reshape(1, d), ffn2_w_gu[l, :, :D_FF].astype(BF16),
                  ffn2_w_gu[l, :, D_FF:].astype(BF16), ffn2_w_down[l].astype(BF16), fin, l == DEPTH - 1)
    return xf.reshape(batch, seq, d)
```

```python
import functools
import math

import jax
import jax.numpy as jnp
import numpy as np
from jax import lax
from jax.experimental import pallas as pl
from jax.experimental.pallas import tpu as pltpu

F32 = jnp.float32
BF16 = jnp.bfloat16

D_MODEL = 1024
D_FF = 2816
DEPTH = 2
EPS = 1e-6
MLA_HEADS = 6
MLA_Q_RANK = 256
MLA_KV_RANK = 128
MLA_NOPE = 64
MLA_ROPE = 32
MLA_V = 64
ROPE_THETA = 10000.0
POOL_WINDOWS = (2, 4, 8, 16)
POOL_GROUP = 64
POOL_WIDTH = 256
FOX_HEADS = 6
FOX_HEAD_DIM = 64
HEADS = MLA_HEADS + FOX_HEADS
HEAD_PAD = 128
V_DIM = 64
LOG2E = math.log2(math.e)
SCALE_MLA = LOG2E / math.sqrt(MLA_NOPE + MLA_ROPE)
SCALE_FOX = LOG2E / math.sqrt(FOX_HEAD_DIM)
POOL_HISTORY = 16
NEG_BIG = -1e30

C_QA = 0
C_KVA = C_QA + MLA_Q_RANK
C_KR = C_KVA + MLA_KV_RANK
C_KRR = C_KR + HEAD_PAD
C_FF = C_KRR + HEAD_PAD
C_POOL = C_FF + HEAD_PAD
C_FQ = C_POOL + POOL_WIDTH
C_FK = C_FQ + FOX_HEADS * HEAD_PAD
C_FV = C_FK + FOX_HEADS * HEAD_PAD
N_IN_PAD = C_FV + FOX_HEADS * HEAD_PAD

VMEM_LIMIT = 56 * 1024 * 1024

FFN_TM = 512
FFN_FC = 1536
MIX_TS = 512
ATT_T = 512
ATT_RC = 64
OUT_TM = 1024


def _rms(x, g):
    return x * lax.rsqrt(jnp.mean(x * x, axis=-1, keepdims=True) + EPS) * g


def _split3(x):
    hi = x.astype(BF16)
    r = x - hi.astype(F32)
    mid = r.astype(BF16)
    lo = (r - mid.astype(F32)).astype(BF16)
    return hi, mid, lo


def _const_spec(shape):
    return pl.BlockSpec(shape, lambda *_: (0,) * len(shape))


def _ffn_kernel(x_ref, g_ref, wgu_ref, wd_ref, post_ref, *refs, fc, mode):
    if mode == "next":
        o_ref, hn_ref, a_ref = refs
    else:
        o_ref, a_ref = refs
    x = x_ref[...]
    h = _rms(x, g_ref[...]).astype(BF16)
    for lo in range(0, D_FF, fc):
        hi = min(lo + fc, D_FF)
        g = jnp.dot(h, wgu_ref[:, lo:hi], preferred_element_type=F32)
        u = jnp.dot(h, wgu_ref[:, D_FF + lo:D_FF + hi], preferred_element_type=F32)
        a_ref[:, lo:hi] = (g * jax.nn.sigmoid(g) * u).astype(BF16)
    y = x + 0.5 * jnp.dot(a_ref[...], wd_ref[...], preferred_element_type=F32)
    if mode == "final":
        y = _rms(y, post_ref[...])
    o_ref[...] = y
    if mode == "next":
        hn_ref[...] = _rms(y, post_ref[...]).astype(BF16)


def _ffn(x, g, wgu, wd, layer, post, mode):
    t = x.shape[0]
    tm = FFN_TM
    tile = pl.BlockSpec((tm, D_MODEL), lambda i: (i, 0))
    out_specs = [tile]
    out_shape = [jax.ShapeDtypeStruct((t, D_MODEL), F32)]
    if mode == "next":
        out_specs.append(tile)
        out_shape.append(jax.ShapeDtypeStruct((t, D_MODEL), BF16))
    return pl.pallas_call(
        functools.partial(_ffn_kernel, fc=FFN_FC, mode=mode),
        grid=(t // tm,),
        in_specs=[
            tile,
            _const_spec((1, D_MODEL)),
            pl.BlockSpec((None, D_MODEL, 2 * D_FF), lambda i: (layer, 0, 0)),
            pl.BlockSpec((None, D_FF, D_MODEL), lambda i: (layer, 0, 0)),
            _const_spec((1, D_MODEL)),
        ],
        out_specs=out_specs,
        out_shape=out_shape,
        scratch_shapes=[pltpu.VMEM((tm, D_FF), BF16)],
        compiler_params=pltpu.CompilerParams(
            dimension_semantics=("arbitrary",), vmem_limit_bytes=VMEM_LIMIT),
        name="ffn",
    )(x, g, wgu, wd, post)


def _mix_kernel(h_ref, win_ref, qan_ref, wq_ref, kvn_ref, wkv_ref, cos_ref, sin_ref,
                pbd_ref, pscale_ref, fb_ref, e_ref, oq_ref, ok_ref, ov_ref,
                q_out, k_out, v_out, yb_out, hist_ref, carry_ref, *, ts):
    si = pl.program_id(1)

    @pl.when(si == 0)
    def _():
        hist_ref[...] = jnp.zeros_like(hist_ref)
        carry_ref[...] = jnp.zeros_like(carry_ref)

    h = h_ref[...]

    def proj(lo, hi):
        return jnp.dot(h, win_ref[:, lo:hi], preferred_element_type=F32)

    cos_t = cos_ref[...]
    sin_t = sin_ref[...]
    nq = MLA_HEADS * HEAD_PAD

    z_krr_ff = proj(C_KRR, C_POOL)
    z_kv_kr = proj(C_KVA, C_KRR)
    z_qa = proj(C_QA, C_KVA)
    u = proj(C_POOL, C_FQ)
    nf = FOX_HEADS * HEAD_PAD
    ones_v = ov_ref[...]
    v_out[:, nq:nq + nf] = (proj(C_FV, N_IN_PAD) + ones_v).astype(BF16)
    z_fq = proj(C_FQ, C_FK)
    z_fk = proj(C_FK, C_FV)

    hq = _rms(z_qa, qan_ref[...]).astype(BF16)
    zq = jnp.dot(hq, wq_ref[...], preferred_element_type=F32)
    for hh in range(MLA_HEADS):
        a = zq[:, hh * HEAD_PAD:(hh + 1) * HEAD_PAD]
        b = zq[:, nq + hh * HEAD_PAD:nq + (hh + 1) * HEAD_PAD]
        q_out[:, hh * HEAD_PAD:(hh + 1) * HEAD_PAD] = ((a * cos_t + b * sin_t) * SCALE_MLA).astype(BF16)

    hkv = _rms(z_kv_kr[:, :MLA_KV_RANK], kvn_ref[...]).astype(BF16)
    zkv = jnp.dot(hkv, wkv_ref[...], preferred_element_type=F32)
    kpe = z_kv_kr[:, MLA_KV_RANK:] * cos_t + z_krr_ff[:, :HEAD_PAD] * sin_t
    for hh in range(MLA_HEADS):
        k_out[:, hh * HEAD_PAD:(hh + 1) * HEAD_PAD] = (zkv[:, hh * HEAD_PAD:(hh + 1) * HEAD_PAD] + kpe).astype(BF16)
    v_out[:, 0:nq] = (zkv[:, nq:2 * nq] + ones_v).astype(BF16)

    e = jnp.concatenate([hist_ref[...], u], axis=0)
    s2 = e + pltpu.roll(e, 1, 0)
    s4 = s2 + pltpu.roll(s2, 2, 0)
    s8 = s4 + pltpu.roll(s4, 4, 0)
    s16 = s8 + pltpu.roll(s8, 8, 0)
    hist_ref[...] = u[ts - POOL_HISTORY:, :]
    lane = lax.broadcasted_iota(jnp.int32, (ts, POOL_WIDTH), 1)
    row = lax.broadcasted_iota(jnp.int32, (ts, POOL_WIDTH), 0)
    g_id = lane // POOL_GROUP
    wsum = jnp.where(g_id == 0, s2[POOL_HISTORY:], jnp.where(g_id == 1, s4[POOL_HISTORY:],
                     jnp.where(g_id == 2, s8[POOL_HISTORY:], s16[POOL_HISTORY:])))
    width = jnp.where(g_id == 0, 2, jnp.where(g_id == 1, 4, jnp.where(g_id == 2, 8, 16)))
    count = jnp.minimum(si * ts + row + 1, width).astype(F32)
    pooled = (wsum / count - u).astype(BF16)
    yb = jnp.dot(pooled, pbd_ref[...], preferred_element_type=F32) * pscale_ref[...]
    yb_out[...] = yb.astype(BF16)

    zf = z_krr_ff[:, HEAD_PAD:] + fb_ref[...]
    cum = -(jnp.maximum(-zf, 0.0) + jnp.log1p(jnp.exp(-jnp.abs(zf))))
    g_row = lax.broadcasted_iota(jnp.int32, (ts, HEAD_PAD), 0)
    shift = 1
    while shift < ts:
        cum = cum + jnp.where(g_row >= shift, pltpu.roll(cum, shift, 0), 0.0)
        shift *= 2
    cum = cum + carry_ref[...]
    carry_ref[...] = cum[ts - 1:ts, :]
    g_lane = lax.broadcasted_iota(jnp.int32, (ts, HEAD_PAD), 1)
    c0 = jnp.where(g_lane < FOX_HEADS, cum * LOG2E, 0.0)
    c_rep = c0 + pltpu.roll(c0, FOX_HEADS, 1) + pltpu.roll(c0, 2 * FOX_HEADS, 1)
    c_hi, c_mid, c_lo = _split3(c_rep)
    c_parts = jnp.where(g_lane < FOX_HEADS, c_hi, jnp.where(g_lane < 2 * FOX_HEADS, c_mid, c_lo))
    ce = jnp.dot(c_parts, e_ref[...], preferred_element_type=F32)
    q_out[:, nq:nq + nf] = (z_fq * SCALE_FOX + ce[:, :nf] + oq_ref[...]).astype(BF16)
    k_out[:, nq:nq + nf] = (z_fk + ce[:, nf:] + ok_ref[...]).astype(BF16)


def _mix(h, win, qan, wq, kvn, wkv, cos_t, sin_t, pbd, pscale, fb, emat, oq, ok, ov, batch, seq):
    t = h.shape[0]
    ts = MIX_TS
    ns = seq // ts
    row = lambda b, s: (b * ns + s, 0)
    tab = lambda b, s: (s, 0)
    return pl.pallas_call(
        functools.partial(_mix_kernel, ts=ts),
        grid=(batch, ns),
        in_specs=[
            pl.BlockSpec((ts, D_MODEL), row),
            _const_spec((D_MODEL, N_IN_PAD)),
            _const_spec((1, MLA_Q_RANK)),
            _const_spec(wq.shape),
            _const_spec((1, MLA_KV_RANK)),
            _const_spec(wkv.shape),
            pl.BlockSpec((ts, HEAD_PAD), tab),
            pl.BlockSpec((ts, HEAD_PAD), tab),
            _const_spec(pbd.shape),
            _const_spec((1, POOL_WIDTH)),
            _const_spec((1, HEAD_PAD)),
            _const_spec(emat.shape),
            _const_spec(oq.shape),
            _const_spec(ok.shape),
            _const_spec(ov.shape),
        ],
        out_specs=[
            pl.BlockSpec((ts, HEADS * HEAD_PAD), row),
            pl.BlockSpec((ts, HEADS * HEAD_PAD), row),
            pl.BlockSpec((ts, HEADS * HEAD_PAD), row),
            pl.BlockSpec((ts, POOL_WIDTH), row),
        ],
        out_shape=[
            jax.ShapeDtypeStruct((t, HEADS * HEAD_PAD), BF16),
            jax.ShapeDtypeStruct((t, HEADS * HEAD_PAD), BF16),
            jax.ShapeDtypeStruct((t, HEADS * HEAD_PAD), BF16),
            jax.ShapeDtypeStruct((t, POOL_WIDTH), BF16),
        ],
        scratch_shapes=[pltpu.VMEM((POOL_HISTORY, POOL_WIDTH), F32), pltpu.VMEM((1, HEAD_PAD), F32)],
        compiler_params=pltpu.CompilerParams(
            dimension_semantics=("arbitrary", "arbitrary"), vmem_limit_bytes=VMEM_LIMIT),
        name="mixproj",
    )(h, win, qan, wq, kvn, wkv, cos_t, sin_t, pbd, pscale, fb, emat, oq, ok, ov)


def _attn_kernel(q_ref, k_ref, v_ref, o_ref, s_ref, p_ref, m_ref, al_ref, acc_ref, *, t, rc, nq):
    nkb = t // HEAD_PAD
    head_cols = [slice(hh * HEAD_PAD, (hh + 1) * HEAD_PAD) for hh in range(2)]

    def rows_of(blk):
        return pl.ds(pl.multiple_of(blk * t, t), t)

    def scores(i, j, hh):
        s_ref[hh] = lax.dot_general(q_ref[rows_of(i), head_cols[hh]], k_ref[rows_of(j), head_cols[hh]],
                                    (((1,), (1,)), ((), ())), preferred_element_type=F32)

    def softmax(hh, masked, first):
        for c in range(t // rc):
            r0 = c * rc
            rows = slice(r0, r0 + rc)
            live = (r0 + rc - 1) // HEAD_PAD + 1 if masked else nkb
            blocks = []
            for kb in range(live):
                blk = s_ref[hh, rows, kb * HEAD_PAD:(kb + 1) * HEAD_PAD]
                if masked and (kb + 1) * HEAD_PAD - 1 > r0:
                    r_i = r0 + lax.broadcasted_iota(jnp.int32, (rc, HEAD_PAD), 0)
                    c_i = kb * HEAD_PAD + lax.broadcasted_iota(jnp.int32, (rc, HEAD_PAD), 1)
                    blk = jnp.where(c_i <= r_i, blk, NEG_BIG)
                blocks.append(blk)
            bmax = blocks[0]
            for blk in blocks[1:]:
                bmax = jnp.maximum(bmax, blk)
            rmax = jnp.max(bmax, axis=-1, keepdims=True)
            m_old = jnp.full((rc, HEAD_PAD), NEG_BIG, F32) if first else m_ref[hh, rows, :]
            m_new = jnp.maximum(m_old, rmax)
            al_ref[hh, rows, :] = jnp.exp2(m_old - m_new)
            m_ref[hh, rows, :] = m_new
            for kb in range(nkb):
                pcols = slice(kb * HEAD_PAD, (kb + 1) * HEAD_PAD)
                if kb < live:
                    p_ref[hh, rows, pcols] = jnp.exp2(blocks[kb] - m_new).astype(BF16)
                else:
                    p_ref[hh, rows, pcols] = (blocks[0] * 0.0).astype(BF16)

    def accumulate(j, hh):
        pv = jnp.dot(p_ref[hh], v_ref[rows_of(j), head_cols[hh]], preferred_element_type=F32)
        acc_ref[hh] = al_ref[hh] * acc_ref[hh] + pv

    def finish(i):
        outs = []
        for hh in range(2):
            acc = acc_ref[hh]
            outs.append(acc[:, :V_DIM] / acc[:, V_DIM:V_DIM + 1])
        o_ref[rows_of(i), :] = jnp.concatenate(outs, axis=1).astype(BF16)

    for hh in range(2):
        zero = v_ref[0:t, head_cols[hh]].astype(F32) * 0.0
        acc_ref[hh] = zero
        if hh == 1:
            m_ref[hh] = zero + NEG_BIG
    scores(0, 0, 0)
    scores(0, 0, 1)
    softmax(0, True, True)

    def stage(i, j, ni, nj, cur_diag, next_diag):
        accumulate(j, 0)
        scores(ni, nj, 0)
        softmax(1, cur_diag, False)
        scores(ni, nj, 1)
        softmax(0, next_diag, cur_diag)
        accumulate(j, 1)
        if cur_diag:
            finish(i)
            m_ref[1] = m_ref[1] * 0.0 + NEG_BIG

    def body(n, carry):
        i, j = carry
        cur_diag = j == i
        ni = jnp.where(cur_diag, i + 1, i)
        nj = jnp.where(cur_diag, 0, j + 1)

        @pl.when(cur_diag)
        def _():
            stage(i, j, ni, nj, True, False)

        @pl.when(j + 1 == i)
        def _():
            stage(i, j, ni, nj, False, True)

        @pl.when(j + 1 < i)
        def _():
            stage(i, j, ni, nj, False, False)

        return ni, nj

    n_steps = nq * (nq + 1) // 2
    last_i, last_j = lax.fori_loop(0, n_steps - 1, body, (jnp.int32(0), jnp.int32(0)))
    softmax(1, True, False)
    accumulate(last_j, 0)
    accumulate(last_j, 1)
    finish(last_i)


def _attention(q, k, v, batch, seq):
    t = ATT_T
    nq = seq // t
    pairs = HEADS // 2
    whole = lambda b, p: (b, p)
    return pl.pallas_call(
        functools.partial(_attn_kernel, t=t, rc=ATT_RC, nq=nq),
        grid=(batch, pairs),
        in_specs=[
            pl.BlockSpec((seq, 2 * HEAD_PAD), whole),
            pl.BlockSpec((seq, 2 * HEAD_PAD), whole),
            pl.BlockSpec((seq, 2 * HEAD_PAD), whole),
        ],
        out_specs=pl.BlockSpec((seq, 2 * V_DIM), whole),
        out_shape=jax.ShapeDtypeStruct((batch * seq, HEADS * V_DIM), BF16),
        scratch_shapes=[
            pltpu.VMEM((2, t, t), F32),
            pltpu.VMEM((2, t, t), BF16),
            pltpu.VMEM((2, t, HEAD_PAD), F32),
            pltpu.VMEM((2, t, HEAD_PAD), F32),
            pltpu.VMEM((2, t, HEAD_PAD), F32),
        ],
        compiler_params=pltpu.CompilerParams(
            dimension_semantics=("arbitrary", "arbitrary"), vmem_limit_bytes=VMEM_LIMIT),
        name="attention",
    )(q, k, v)


def _out_kernel(x_ref, o_ref, yb_ref, wo_ref, wb_ref, out_ref):
    out_ref[...] = (x_ref[...]
                    + jnp.dot(o_ref[...], wo_ref[...], preferred_element_type=F32)
                    + jnp.dot(yb_ref[...], wb_ref[...], preferred_element_type=F32))


def _outproj(x, o, yb, wo, wb):
    t = x.shape[0]
    tm = OUT_TM
    return pl.pallas_call(
        _out_kernel,
        grid=(t // tm,),
        in_specs=[
            pl.BlockSpec((tm, D_MODEL), lambda i: (i, 0)),
            pl.BlockSpec((tm, HEADS * V_DIM), lambda i: (i, 0)),
            pl.BlockSpec((tm, POOL_WIDTH), lambda i: (i, 0)),
            _const_spec(wo.shape),
            _const_spec(wb.shape),
        ],
        out_specs=pl.BlockSpec((tm, D_MODEL), lambda i: (i, 0)),
        out_shape=jax.ShapeDtypeStruct((t, D_MODEL), F32),
        compiler_params=pltpu.CompilerParams(
            dimension_semantics=("arbitrary",), vmem_limit_bytes=VMEM_LIMIT),
        name="outproj",
    )(x, o, yb, wo, wb)


def _zeros(rows, cols):
    return jnp.zeros((rows, cols), BF16)


def _rot_pieces(w):
    half = w.shape[-1] // 2
    return [-w[:, half:], w[:, :half]]


def _prep_mix_weights(w_in, w_q_b, w_kv_b, pool_w, w_out):
    o2 = MLA_Q_RANK + MLA_KV_RANK
    o3 = o2 + MLA_ROPE
    o4 = o3 + POOL_WIDTH
    hd = FOX_HEADS * FOX_HEAD_DIM
    o5 = o4 + 3 * hd
    d = w_in.shape[0]
    rope_tail = HEAD_PAD - MLA_NOPE - MLA_ROPE
    w_kr = w_in[:, o2:o3]
    pieces = [w_in[:, 0:o2],
              _zeros(d, MLA_NOPE), w_kr, _zeros(d, rope_tail),
              _zeros(d, MLA_NOPE), *_rot_pieces(w_kr), _zeros(d, rope_tail),
              w_in[:, o5:], _zeros(d, HEAD_PAD - FOX_HEADS),
              w_in[:, o3:o4]]
    for part in range(3):
        for hh in range(FOX_HEADS):
            lo = o4 + part * hd + hh * FOX_HEAD_DIM
            pieces += [w_in[:, lo:lo + FOX_HEAD_DIM], _zeros(d, HEAD_PAD - FOX_HEAD_DIM)]
    win = jnp.concatenate(pieces, axis=1)
    assert win.shape == (d, N_IN_PAD)

    qd = MLA_NOPE + MLA_ROPE
    plain, rot = [], []
    for hh in range(MLA_HEADS):
        plain += [w_q_b[:, hh * qd:(hh + 1) * qd], _zeros(MLA_Q_RANK, rope_tail)]
        rot += [_zeros(MLA_Q_RANK, MLA_NOPE), *_rot_pieces(w_q_b[:, hh * qd + MLA_NOPE:(hh + 1) * qd]),
                _zeros(MLA_Q_RANK, rope_tail)]
    wq = jnp.concatenate(plain + rot, axis=1)

    kd = MLA_NOPE + MLA_V
    keys, vals = [], []
    for hh in range(MLA_HEADS):
        keys += [w_kv_b[:, hh * kd:hh * kd + MLA_NOPE], _zeros(MLA_KV_RANK, HEAD_PAD - MLA_NOPE)]
        vals += [w_kv_b[:, hh * kd + MLA_NOPE:(hh + 1) * kd], _zeros(MLA_KV_RANK, HEAD_PAD - MLA_V)]
    wkv = jnp.concatenate(keys + vals, axis=1)

    ng = len(POOL_WINDOWS)
    pbd = jnp.concatenate([
        jnp.concatenate([pool_w[g] if c == g else _zeros(POOL_GROUP, POOL_GROUP) for c in range(ng)], axis=1)
        for g in range(ng)], axis=0)

    na = MLA_HEADS * MLA_V
    wo = jnp.concatenate([w_out[:na], w_out[na + POOL_WIDTH:]], axis=0)
    wb = w_out[na:na + POOL_WIDTH]
    return win, wq, wkv, pbd, wo, wb


def _gate_lane_constants():
    nf = FOX_HEADS * HEAD_PAD
    e = [[0.0] * (2 * nf) for _ in range(HEAD_PAD)]
    oq = [0.0] * nf
    ok = [0.0] * nf
    ov = [0.0] * nf
    for hh in range(FOX_HEADS):
        ov[hh * HEAD_PAD + V_DIM] = 1.0
        for part in range(3):
            e[part * FOX_HEADS + hh][hh * HEAD_PAD + FOX_HEAD_DIM + part] = 1.0
            e[part * FOX_HEADS + hh][nf + hh * HEAD_PAD + FOX_HEAD_DIM + 3 + part] = -1.0
            oq[hh * HEAD_PAD + FOX_HEAD_DIM + 3 + part] = 1.0
            ok[hh * HEAD_PAD + FOX_HEAD_DIM + part] = 1.0
    row = lambda r: jnp.array(r, F32).reshape(1, nf)
    return jnp.array(e, F32).astype(BF16), row(oq), row(ok), row(ov)


def _rope_tables(seq):
    r = MLA_ROPE
    inv_freq = ROPE_THETA ** (-np.arange(0, r, 2, dtype=np.float64) / r)
    ang = np.arange(seq, dtype=np.float64)[:, None] * inv_freq[None, :]
    cos = np.cos(ang).astype(np.float32)
    sin = np.sin(ang).astype(np.float32)
    tail = HEAD_PAD - MLA_NOPE - MLA_ROPE
    cos_t = np.concatenate([np.ones((seq, MLA_NOPE), np.float32), cos, cos, np.zeros((seq, tail), np.float32)], axis=1)
    sin_t = np.concatenate([np.zeros((seq, MLA_NOPE), np.float32), sin, sin, np.zeros((seq, tail), np.float32)], axis=1)
    return jnp.asarray(cos_t), jnp.asarray(sin_t)


def kernel(x, ffn1_norm, ffn1_w_gu, ffn1_w_down, mix_norm, w_in, q_a_norm, w_q_b, kv_a_norm, w_kv_b, pool_w, pool_scale, fox_b_f, w_out, ffn2_norm, ffn2_w_gu, ffn2_w_down, final_norm):
    batch, seq, d = x.shape
    xf = x.reshape(batch * seq, d)
    cos_t, sin_t = _rope_tables(seq)
    emat, oq, ok, ov = _gate_lane_constants()
    fin = final_norm.reshape(1, d)
    wgu1, wd1 = ffn1_w_gu.astype(BF16), ffn1_w_down.astype(BF16)
    wgu2, wd2 = ffn2_w_gu.astype(BF16), ffn2_w_down.astype(BF16)
    w_in_b, w_q_b_b, w_kv_b_b = w_in.astype(BF16), w_q_b.astype(BF16), w_kv_b.astype(BF16)
    pool_w_b, w_out_b = pool_w.astype(BF16), w_out.astype(BF16)
    for l in range(DEPTH):
        xf, hmix = _ffn(xf, ffn1_norm[l].reshape(1, d), wgu1, wd1, l, mix_norm[l].reshape(1, d), "next")
        win, wq, wkv, pbd, wo, wb = _prep_mix_weights(w_in_b[l], w_q_b_b[l], w_kv_b_b[l], pool_w_b[l], w_out_b[l])
        fb = jnp.pad(fox_b_f[l], (0, HEAD_PAD - FOX_HEADS)).reshape(1, HEAD_PAD)
        q, k, v, yb = _mix(hmix, win, q_a_norm[l].reshape(1, -1), wq, kv_a_norm[l].reshape(1, -1), wkv,
                           cos_t, sin_t, pbd, pool_scale[l].reshape(1, -1), fb, emat, oq, ok, ov, batch, seq)
        o = _attention(q, k, v, batch, seq)
        xf = _outproj(xf, o, yb, wo, wb)
        last = l == DEPTH - 1
        xf = _ffn(xf, ffn2_norm[l].reshape(1, d), wgu2, wd2, l, fin, "final" if last else "plain")[0]
    return xf.reshape(batch, seq, d)
```

```python
import functools
import math

import jax
import jax.numpy as jnp
import numpy as np
from jax import lax
from jax.experimental import pallas as pl
from jax.experimental.pallas import tpu as pltpu

F32 = jnp.float32
BF16 = jnp.bfloat16

D_MODEL = 1024
D_FF = 2816
DEPTH = 2
EPS = 1e-6
MLA_HEADS = 6
MLA_Q_RANK = 256
MLA_KV_RANK = 128
MLA_NOPE = 64
MLA_ROPE = 32
MLA_V = 64
ROPE_THETA = 10000.0
POOL_WINDOWS = (2, 4, 8, 16)
POOL_GROUP = 64
POOL_WIDTH = 256
FOX_HEADS = 6
FOX_HEAD_DIM = 64
HEADS = MLA_HEADS + FOX_HEADS
HEAD_PAD = 128
V_DIM = 64
LOG2E = math.log2(math.e)
SCALE_MLA = LOG2E / math.sqrt(MLA_NOPE + MLA_ROPE)
SCALE_FOX = LOG2E / math.sqrt(FOX_HEAD_DIM)
POOL_HISTORY = 16
NEG_BIG = -1e30

C_QA = 0
C_KVA = C_QA + MLA_Q_RANK
C_KR = C_KVA + MLA_KV_RANK
C_KRR = C_KR + HEAD_PAD
C_FF = C_KRR + HEAD_PAD
C_POOL = C_FF + HEAD_PAD
C_FQ = C_POOL + POOL_WIDTH
C_FK = C_FQ + FOX_HEADS * HEAD_PAD
C_FV = C_FK + FOX_HEADS * HEAD_PAD
N_IN_PAD = C_FV + FOX_HEADS * HEAD_PAD

VMEM_LIMIT = 56 * 1024 * 1024

FFN_TM = 512
FFN_FC = 1536
MIX_TS = 512
ATT_T = 512
ATT_RC = 64
OUT_TM = 1024


def _rms(x, g):
    return x * lax.rsqrt(jnp.mean(x * x, axis=-1, keepdims=True) + EPS) * g


def _split3(x):
    hi = x.astype(BF16)
    r = x - hi.astype(F32)
    mid = r.astype(BF16)
    lo = (r - mid.astype(F32)).astype(BF16)
    return hi, mid, lo


def _const_spec(shape):
    return pl.BlockSpec(shape, lambda *_: (0,) * len(shape))


def _ffn_kernel(x_ref, g_ref, wgu_ref, wd_ref, post_ref, *refs, fc, mode):
    if mode == "next":
        o_ref, hn_ref, a_ref = refs
    else:
        o_ref, a_ref = refs
    x = x_ref[...]
    h = _rms(x, g_ref[...]).astype(BF16)
    for lo in range(0, D_FF, fc):
        hi = min(lo + fc, D_FF)
        g = jnp.dot(h, wgu_ref[:, lo:hi], preferred_element_type=F32)
        u = jnp.dot(h, wgu_ref[:, D_FF + lo:D_FF + hi], preferred_element_type=F32)
        a_ref[:, lo:hi] = (g * jax.nn.sigmoid(g) * u).astype(BF16)
    y = x + 0.5 * jnp.dot(a_ref[...], wd_ref[...], preferred_element_type=F32)
    if mode == "final":
        y = _rms(y, post_ref[...])
    o_ref[...] = y
    if mode == "next":
        hn_ref[...] = _rms(y, post_ref[...]).astype(BF16)


def _ffn(x, g, wgu, wd, layer, post, mode):
    t = x.shape[0]
    tm = FFN_TM
    tile = pl.BlockSpec((tm, D_MODEL), lambda i: (i, 0))
    out_specs = [tile]
    out_shape = [jax.ShapeDtypeStruct((t, D_MODEL), F32)]
    if mode == "next":
        out_specs.append(tile)
        out_shape.append(jax.ShapeDtypeStruct((t, D_MODEL), BF16))
    return pl.pallas_call(
        functools.partial(_ffn_kernel, fc=FFN_FC, mode=mode),
        grid=(t // tm,),
        in_specs=[
            tile,
            _const_spec((1, D_MODEL)),
            pl.BlockSpec((None, D_MODEL, 2 * D_FF), lambda i: (layer, 0, 0)),
            pl.BlockSpec((None, D_FF, D_MODEL), lambda i: (layer, 0, 0)),
            _const_spec((1, D_MODEL)),
        ],
        out_specs=out_specs,
        out_shape=out_shape,
        scratch_shapes=[pltpu.VMEM((tm, D_FF), BF16)],
        compiler_params=pltpu.CompilerParams(
            dimension_semantics=("arbitrary",), vmem_limit_bytes=VMEM_LIMIT),
        name="ffn",
    )(x, g, wgu, wd, post)


def _mix_kernel(h_ref, win_ref, qan_ref, wq_ref, kvn_ref, wkv_ref, cos_ref, sin_ref,
                pbd_ref, pscale_ref, fb_ref, e_ref, oq_ref, ok_ref, ov_ref,
                q_out, k_out, v_out, yb_out, hist_ref, carry_ref, *, ts):
    si = pl.program_id(1)

    @pl.when(si == 0)
    def _():
        hist_ref[...] = jnp.zeros_like(hist_ref)
        carry_ref[...] = jnp.zeros_like(carry_ref)

    h = h_ref[...]

    def proj(lo, hi):
        return jnp.dot(h, win_ref[:, lo:hi], preferred_element_type=F32)

    cos_t = cos_ref[...]
    sin_t = sin_ref[...]
    nq = MLA_HEADS * HEAD_PAD

    z_krr_ff = proj(C_KRR, C_POOL)
    z_kv_kr = proj(C_KVA, C_KRR)
    z_qa = proj(C_QA, C_KVA)
    u = proj(C_POOL, C_FQ)
    nf = FOX_HEADS * HEAD_PAD
    ones_v = ov_ref[...]
    v_out[:, nq:nq + nf] = (proj(C_FV, N_IN_PAD) + ones_v).astype(BF16)
    z_fq = proj(C_FQ, C_FK)
    z_fk = proj(C_FK, C_FV)

    hq = _rms(z_qa, qan_ref[...]).astype(BF16)
    zq = jnp.dot(hq, wq_ref[...], preferred_element_type=F32)
    for hh in range(MLA_HEADS):
        a = zq[:, hh * HEAD_PAD:(hh + 1) * HEAD_PAD]
        b = zq[:, nq + hh * HEAD_PAD:nq + (hh + 1) * HEAD_PAD]
        q_out[:, hh * HEAD_PAD:(hh + 1) * HEAD_PAD] = ((a * cos_t + b * sin_t) * SCALE_MLA).astype(BF16)

    hkv = _rms(z_kv_kr[:, :MLA_KV_RANK], kvn_ref[...]).astype(BF16)
    zkv = jnp.dot(hkv, wkv_ref[...], preferred_element_type=F32)
    kpe = z_kv_kr[:, MLA_KV_RANK:] * cos_t + z_krr_ff[:, :HEAD_PAD] * sin_t
    for hh in range(MLA_HEADS):
        k_out[:, hh * HEAD_PAD:(hh + 1) * HEAD_PAD] = (zkv[:, hh * HEAD_PAD:(hh + 1) * HEAD_PAD] + kpe).astype(BF16)
    v_out[:, 0:nq] = (zkv[:, nq:2 * nq] + ones_v).astype(BF16)

    e = jnp.concatenate([hist_ref[...], u], axis=0)
    s2 = e + pltpu.roll(e, 1, 0)
    s4 = s2 + pltpu.roll(s2, 2, 0)
    s8 = s4 + pltpu.roll(s4, 4, 0)
    s16 = s8 + pltpu.roll(s8, 8, 0)
    hist_ref[...] = u[ts - POOL_HISTORY:, :]
    lane = lax.broadcasted_iota(jnp.int32, (ts, POOL_WIDTH), 1)
    row = lax.broadcasted_iota(jnp.int32, (ts, POOL_WIDTH), 0)
    g_id = lane // POOL_GROUP
    wsum = jnp.where(g_id == 0, s2[POOL_HISTORY:], jnp.where(g_id == 1, s4[POOL_HISTORY:],
                     jnp.where(g_id == 2, s8[POOL_HISTORY:], s16[POOL_HISTORY:])))
    width = jnp.where(g_id == 0, 2, jnp.where(g_id == 1, 4, jnp.where(g_id == 2, 8, 16)))
    count = jnp.minimum(si * ts + row + 1, width).astype(F32)
    pooled = (wsum / count - u).astype(BF16)
    yb = jnp.dot(pooled, pbd_ref[...], preferred_element_type=F32) * pscale_ref[...]
    yb_out[...] = yb.astype(BF16)

    zf = z_krr_ff[:, HEAD_PAD:] + fb_ref[...]
    cum = -(jnp.maximum(-zf, 0.0) + jnp.log1p(jnp.exp(-jnp.abs(zf))))
    g_row = lax.broadcasted_iota(jnp.int32, (ts, HEAD_PAD), 0)
    shift = 1
    while shift < ts:
        cum = cum + jnp.where(g_row >= shift, pltpu.roll(cum, shift, 0), 0.0)
        shift *= 2
    cum = cum + carry_ref[...]
    carry_ref[...] = cum[ts - 1:ts, :]
    g_lane = lax.broadcasted_iota(jnp.int32, (ts, HEAD_PAD), 1)
    c0 = jnp.where(g_lane < FOX_HEADS, cum * LOG2E, 0.0)
    c_rep = c0 + pltpu.roll(c0, FOX_HEADS, 1) + pltpu.roll(c0, 2 * FOX_HEADS, 1)
    c_hi, c_mid, c_lo = _split3(c_rep)
    c_parts = jnp.where(g_lane < FOX_HEADS, c_hi, jnp.where(g_lane < 2 * FOX_HEADS, c_mid, c_lo))
    ce = jnp.dot(c_parts, e_ref[...], preferred_element_type=F32)
    q_out[:, nq:nq + nf] = (z_fq * SCALE_FOX + ce[:, :nf] + oq_ref[...]).astype(BF16)
    k_out[:, nq:nq + nf] = (z_fk + ce[:, nf:] + ok_ref[...]).astype(BF16)


def _mix(h, win, qan, wq, kvn, wkv, cos_t, sin_t, pbd, pscale, fb, emat, oq, ok, ov, batch, seq):
    t = h.shape[0]
    ts = MIX_TS
    ns = seq // ts
    row = lambda b, s: (b * ns + s, 0)
    tab = lambda b, s: (s, 0)
    return pl.pallas_call(
        functools.partial(_mix_kernel, ts=ts),
        grid=(batch, ns),
        in_specs=[
            pl.BlockSpec((ts, D_MODEL), row),
            _const_spec((D_MODEL, N_IN_PAD)),
            _const_spec((1, MLA_Q_RANK)),
            _const_spec(wq.shape),
            _const_spec((1, MLA_KV_RANK)),
            _const_spec(wkv.shape),
            pl.BlockSpec((ts, HEAD_PAD), tab),
            pl.BlockSpec((ts, HEAD_PAD), tab),
            _const_spec(pbd.shape),
            _const_spec((1, POOL_WIDTH)),
            _const_spec((1, HEAD_PAD)),
            _const_spec(emat.shape),
            _const_spec(oq.shape),
            _const_spec(ok.shape),
            _const_spec(ov.shape),
        ],
        out_specs=[
            pl.BlockSpec((ts, HEADS * HEAD_PAD), row),
            pl.BlockSpec((ts, HEADS * HEAD_PAD), row),
            pl.BlockSpec((ts, HEADS * HEAD_PAD), row),
            pl.BlockSpec((ts, POOL_WIDTH), row),
        ],
        out_shape=[
            jax.ShapeDtypeStruct((t, HEADS * HEAD_PAD), BF16),
            jax.ShapeDtypeStruct((t, HEADS * HEAD_PAD), BF16),
            jax.ShapeDtypeStruct((t, HEADS * HEAD_PAD), BF16),
            jax.ShapeDtypeStruct((t, POOL_WIDTH), BF16),
        ],
        scratch_shapes=[pltpu.VMEM((POOL_HISTORY, POOL_WIDTH), F32), pltpu.VMEM((1, HEAD_PAD), F32)],
        compiler_params=pltpu.CompilerParams(
            dimension_semantics=("arbitrary", "arbitrary"), vmem_limit_bytes=VMEM_LIMIT),
        name="mixproj",
    )(h, win, qan, wq, kvn, wkv, cos_t, sin_t, pbd, pscale, fb, emat, oq, ok, ov)


def _attn_kernel(q_ref, k_ref, v_ref, o_ref, s_ref, p_ref, m_ref, al_ref, acc_ref, *, t, rc, nq):
    nkb = t // HEAD_PAD
    head_cols = [slice(hh * HEAD_PAD, (hh + 1) * HEAD_PAD) for hh in range(2)]

    def rows_of(blk):
        return pl.ds(pl.multiple_of(blk * t, t), t)

    half = t // 2

    def scores(i, j, hh, diag):
        dims = (((1,), (1,)), ((), ()))
        if diag:
            q_top = q_ref[pl.ds(pl.multiple_of(i * t, t), half), head_cols[hh]]
            q_bot = q_ref[pl.ds(pl.multiple_of(i * t + half, half), half), head_cols[hh]]
            k_top = k_ref[pl.ds(pl.multiple_of(j * t, t), half), head_cols[hh]]
            s_ref[hh, 0:half, 0:half] = lax.dot_general(q_top, k_top, dims, preferred_element_type=F32)
            s_ref[hh, half:t, :] = lax.dot_general(q_bot, k_ref[rows_of(j), head_cols[hh]], dims,
                                                   preferred_element_type=F32)
        else:
            s_ref[hh] = lax.dot_general(q_ref[rows_of(i), head_cols[hh]], k_ref[rows_of(j), head_cols[hh]],
                                        dims, preferred_element_type=F32)

    def softmax(hh, masked, first):
        for c in range(t // rc):
            r0 = c * rc
            rows = slice(r0, r0 + rc)
            live = (r0 + rc - 1) // HEAD_PAD + 1 if masked else nkb
            blocks = []
            for kb in range(live):
                blk = s_ref[hh, rows, kb * HEAD_PAD:(kb + 1) * HEAD_PAD]
                if masked and (kb + 1) * HEAD_PAD - 1 > r0:
                    r_i = r0 + lax.broadcasted_iota(jnp.int32, (rc, HEAD_PAD), 0)
                    c_i = kb * HEAD_PAD + lax.broadcasted_iota(jnp.int32, (rc, HEAD_PAD), 1)
                    blk = jnp.where(c_i <= r_i, blk, NEG_BIG)
                blocks.append(blk)
            bmax = blocks[0]
            for blk in blocks[1:]:
                bmax = jnp.maximum(bmax, blk)
            rmax = jnp.max(bmax, axis=-1, keepdims=True)
            m_old = jnp.full((rc, HEAD_PAD), NEG_BIG, F32) if first else m_ref[hh, rows, :]
            m_new = jnp.maximum(m_old, rmax)
            al_ref[hh, rows, :] = jnp.exp2(m_old - m_new)
            m_ref[hh, rows, :] = m_new
            for kb in range(nkb):
                pcols = slice(kb * HEAD_PAD, (kb + 1) * HEAD_PAD)
                if kb < live:
                    p_ref[hh, rows, pcols] = jnp.exp2(blocks[kb] - m_new).astype(BF16)
                elif r0 >= half or (kb + 1) * HEAD_PAD <= half:
                    p_ref[hh, rows, pcols] = (blocks[0] * 0.0).astype(BF16)

    def accumulate(j, hh, diag):
        if diag:
            v_top = v_ref[pl.ds(pl.multiple_of(j * t, t), half), head_cols[hh]]
            pv_top = jnp.dot(p_ref[hh, 0:half, 0:half], v_top, preferred_element_type=F32)
            acc_ref[hh, 0:half, :] = al_ref[hh, 0:half, :] * acc_ref[hh, 0:half, :] + pv_top
            pv_bot = jnp.dot(p_ref[hh, half:t, :], v_ref[rows_of(j), head_cols[hh]], preferred_element_type=F32)
            acc_ref[hh, half:t, :] = al_ref[hh, half:t, :] * acc_ref[hh, half:t, :] + pv_bot
        else:
            pv = jnp.dot(p_ref[hh], v_ref[rows_of(j), head_cols[hh]], preferred_element_type=F32)
            acc_ref[hh] = al_ref[hh] * acc_ref[hh] + pv

    def finish(i):
        outs = []
        for hh in range(2):
            acc = acc_ref[hh]
            outs.append(acc[:, :V_DIM] / acc[:, V_DIM:V_DIM + 1])
        o_ref[rows_of(i), :] = jnp.concatenate(outs, axis=1).astype(BF16)

    for hh in range(2):
        zero = v_ref[0:t, head_cols[hh]].astype(F32) * 0.0
        acc_ref[hh] = zero
        if hh == 1:
            m_ref[hh] = zero + NEG_BIG
    scores(0, 0, 0, True)
    scores(0, 0, 1, True)
    softmax(0, True, True)

    def stage(i, j, ni, nj, cur_diag, next_diag):
        accumulate(j, 0, cur_diag)
        scores(ni, nj, 0, next_diag)
        softmax(1, cur_diag, False)
        scores(ni, nj, 1, next_diag)
        softmax(0, next_diag, cur_diag)
        accumulate(j, 1, cur_diag)
        if cur_diag:
            finish(i)
            m_ref[1] = m_ref[1] * 0.0 + NEG_BIG

    def body(n, carry):
        i, j = carry

        @pl.when(j == i)
        def _():
            stage(i, j, i + 1, 0, True, False)

        @pl.when(j + 1 == i)
        def _():
            stage(i, j, i, j + 1, False, True)

        @pl.when(j + 2 == i)
        def _():
            stage(i, j, i, j + 1, False, False)

        @pl.when(j + 2 < i)
        def _():
            stage(i, j, i, j + 1, False, False)
            stage(i, j + 1, i, j + 2, False, False)

        ni = jnp.where(j == i, i + 1, i)
        nj = jnp.where(j == i, 0, jnp.where(j + 2 < i, j + 2, j + 1))
        return ni, nj

    n_iter, wi, wj = 0, 0, 0
    while (wi, wj) != (nq - 1, nq - 1):
        wi, wj = (wi + 1, 0) if wj == wi else (wi, wj + 2 if wj + 2 < wi else wj + 1)
        n_iter += 1
    last_i, last_j = lax.fori_loop(0, n_iter, body, (jnp.int32(0), jnp.int32(0)))
    softmax(1, True, False)
    accumulate(last_j, 0, True)
    accumulate(last_j, 1, True)
    finish(last_i)


def _attention(q, k, v, batch, seq):
    t = ATT_T
    nq = seq // t
    pairs = HEADS // 2
    whole = lambda b, p: (b, p)
    return pl.pallas_call(
        functools.partial(_attn_kernel, t=t, rc=ATT_RC, nq=nq),
        grid=(batch, pairs),
        in_specs=[
            pl.BlockSpec((seq, 2 * HEAD_PAD), whole),
            pl.BlockSpec((seq, 2 * HEAD_PAD), whole),
            pl.BlockSpec((seq, 2 * HEAD_PAD), whole),
        ],
        out_specs=pl.BlockSpec((seq, 2 * V_DIM), whole),
        out_shape=jax.ShapeDtypeStruct((batch * seq, HEADS * V_DIM), BF16),
        scratch_shapes=[
            pltpu.VMEM((2, t, t), F32),
            pltpu.VMEM((2, t, t), BF16),
            pltpu.VMEM((2, t, HEAD_PAD), F32),
            pltpu.VMEM((2, t, HEAD_PAD), F32),
            pltpu.VMEM((2, t, HEAD_PAD), F32),
        ],
        compiler_params=pltpu.CompilerParams(
            dimension_semantics=("arbitrary", "arbitrary"), vmem_limit_bytes=VMEM_LIMIT),
        name="attention",
    )(q, k, v)


def _out_kernel(x_ref, o_ref, yb_ref, wo_ref, wb_ref, out_ref):
    out_ref[...] = (x_ref[...]
                    + jnp.dot(o_ref[...], wo_ref[...], preferred_element_type=F32)
                    + jnp.dot(yb_ref[...], wb_ref[...], preferred_element_type=F32))


def _outproj(x, o, yb, wo, wb):
    t = x.shape[0]
    tm = OUT_TM
    return pl.pallas_call(
        _out_kernel,
        grid=(t // tm,),
        in_specs=[
            pl.BlockSpec((tm, D_MODEL), lambda i: (i, 0)),
            pl.BlockSpec((tm, HEADS * V_DIM), lambda i: (i, 0)),
            pl.BlockSpec((tm, POOL_WIDTH), lambda i: (i, 0)),
            _const_spec(wo.shape),
            _const_spec(wb.shape),
        ],
        out_specs=pl.BlockSpec((tm, D_MODEL), lambda i: (i, 0)),
        out_shape=jax.ShapeDtypeStruct((t, D_MODEL), F32),
        compiler_params=pltpu.CompilerParams(
            dimension_semantics=("arbitrary",), vmem_limit_bytes=VMEM_LIMIT),
        name="outproj",
    )(x, o, yb, wo, wb)


def _zeros(rows, cols):
    return jnp.zeros((rows, cols), BF16)


def _rot_pieces(w):
    half = w.shape[-1] // 2
    return [-w[:, half:], w[:, :half]]


def _prep_mix_weights(w_in, w_q_b, w_kv_b, pool_w, w_out):
    o2 = MLA_Q_RANK + MLA_KV_RANK
    o3 = o2 + MLA_ROPE
    o4 = o3 + POOL_WIDTH
    hd = FOX_HEADS * FOX_HEAD_DIM
    o5 = o4 + 3 * hd
    d = w_in.shape[0]
    rope_tail = HEAD_PAD - MLA_NOPE - MLA_ROPE
    w_kr = w_in[:, o2:o3]
    pieces = [w_in[:, 0:o2],
              _zeros(d, MLA_NOPE), w_kr, _zeros(d, rope_tail),
              _zeros(d, MLA_NOPE), *_rot_pieces(w_kr), _zeros(d, rope_tail),
              w_in[:, o5:], _zeros(d, HEAD_PAD - FOX_HEADS),
              w_in[:, o3:o4]]
    for part in range(3):
        for hh in range(FOX_HEADS):
            lo = o4 + part * hd + hh * FOX_HEAD_DIM
            pieces += [w_in[:, lo:lo + FOX_HEAD_DIM], _zeros(d, HEAD_PAD - FOX_HEAD_DIM)]
    win = jnp.concatenate(pieces, axis=1)
    assert win.shape == (d, N_IN_PAD)

    qd = MLA_NOPE + MLA_ROPE
    plain, rot = [], []
    for hh in range(MLA_HEADS):
        plain += [w_q_b[:, hh * qd:(hh + 1) * qd], _zeros(MLA_Q_RANK, rope_tail)]
        rot += [_zeros(MLA_Q_RANK, MLA_NOPE), *_rot_pieces(w_q_b[:, hh * qd + MLA_NOPE:(hh + 1) * qd]),
                _zeros(MLA_Q_RANK, rope_tail)]
    wq = jnp.concatenate(plain + rot, axis=1)

    kd = MLA_NOPE + MLA_V
    keys, vals = [], []
    for hh in range(MLA_HEADS):
        keys += [w_kv_b[:, hh * kd:hh * kd + MLA_NOPE], _zeros(MLA_KV_RANK, HEAD_PAD - MLA_NOPE)]
        vals += [w_kv_b[:, hh * kd + MLA_NOPE:(hh + 1) * kd], _zeros(MLA_KV_RANK, HEAD_PAD - MLA_V)]
    wkv = jnp.concatenate(keys + vals, axis=1)

    ng = len(POOL_WINDOWS)
    pbd = jnp.concatenate([
        jnp.concatenate([pool_w[g] if c == g else _zeros(POOL_GROUP, POOL_GROUP) for c in range(ng)], axis=1)
        for g in range(ng)], axis=0)

    na = MLA_HEADS * MLA_V
    wo = jnp.concatenate([w_out[:na], w_out[na + POOL_WIDTH:]], axis=0)
    wb = w_out[na:na + POOL_WIDTH]
    return win, wq, wkv, pbd, wo, wb


def _gate_lane_constants():
    nf = FOX_HEADS * HEAD_PAD
    e = [[0.0] * (2 * nf) for _ in range(HEAD_PAD)]
    oq = [0.0] * nf
    ok = [0.0] * nf
    ov = [0.0] * nf
    for hh in range(FOX_HEADS):
        ov[hh * HEAD_PAD + V_DIM] = 1.0
        for part in range(3):
            e[part * FOX_HEADS + hh][hh * HEAD_PAD + FOX_HEAD_DIM + part] = 1.0
            e[part * FOX_HEADS + hh][nf + hh * HEAD_PAD + FOX_HEAD_DIM + 3 + part] = -1.0
            oq[hh * HEAD_PAD + FOX_HEAD_DIM + 3 + part] = 1.0
            ok[hh * HEAD_PAD + FOX_HEAD_DIM + part] = 1.0
    row = lambda r: jnp.array(r, F32).reshape(1, nf)
    return jnp.array(e, F32).astype(BF16), row(oq), row(ok), row(ov)


def _rope_tables(seq):
    r = MLA_ROPE
    inv_freq = ROPE_THETA ** (-np.arange(0, r, 2, dtype=np.float64) / r)
    ang = np.arange(seq, dtype=np.float64)[:, None] * inv_freq[None, :]
    cos = np.cos(ang).astype(np.float32)
    sin = np.sin(ang).astype(np.float32)
    tail = HEAD_PAD - MLA_NOPE - MLA_ROPE
    cos_t = np.concatenate([np.ones((seq, MLA_NOPE), np.float32), cos, cos, np.zeros((seq, tail), np.float32)], axis=1)
    sin_t = np.concatenate([np.zeros((seq, MLA_NOPE), np.float32), sin, sin, np.zeros((seq, tail), np.float32)], axis=1)
    return jnp.asarray(cos_t), jnp.asarray(sin_t)


def kernel(x, ffn1_norm, ffn1_w_gu, ffn1_w_down, mix_norm, w_in, q_a_norm, w_q_b, kv_a_norm, w_kv_b, pool_w, pool_scale, fox_b_f, w_out, ffn2_norm, ffn2_w_gu, ffn2_w_down, final_norm):
    batch, seq, d = x.shape
    xf = x.reshape(batch * seq, d)
    cos_t, sin_t = _rope_tables(seq)
    emat, oq, ok, ov = _gate_lane_constants()
    fin = final_norm.reshape(1, d)
    wgu1, wd1 = ffn1_w_gu.astype(BF16), ffn1_w_down.astype(BF16)
    wgu2, wd2 = ffn2_w_gu.astype(BF16), ffn2_w_down.astype(BF16)
    w_in_b, w_q_b_b, w_kv_b_b = w_in.astype(BF16), w_q_b.astype(BF16), w_kv_b.astype(BF16)
    pool_w_b, w_out_b = pool_w.astype(BF16), w_out.astype(BF16)
    for l in range(DEPTH):
        xf, hmix = _ffn(xf, ffn1_norm[l].reshape(1, d), wgu1, wd1, l, mix_norm[l].reshape(1, d), "next")
        win, wq, wkv, pbd, wo, wb = _prep_mix_weights(w_in_b[l], w_q_b_b[l], w_kv_b_b[l], pool_w_b[l], w_out_b[l])
        fb = jnp.pad(fox_b_f[l], (0, HEAD_PAD - FOX_HEADS)).reshape(1, HEAD_PAD)
        q, k, v, yb = _mix(hmix, win, q_a_norm[l].reshape(1, -1), wq, kv_a_norm[l].reshape(1, -1), wkv,
                           cos_t, sin_t, pbd, pool_scale[l].reshape(1, -1), fb, emat, oq, ok, ov, batch, seq)
        o = _attention(q, k, v, batch, seq)
        xf = _outproj(xf, o, yb, wo, wb)
        last = l == DEPTH - 1
        xf = _ffn(xf, ffn2_norm[l].reshape(1, d), wgu2, wd2, l, fin, "final" if last else "plain")[0]
    return xf.reshape(batch, seq, d)
```

```python
import functools
import math

import jax
import jax.numpy as jnp
import numpy as np
from jax import lax
from jax.experimental import pallas as pl
from jax.experimental.pallas import tpu as pltpu

F32 = jnp.float32
BF16 = jnp.bfloat16

D_MODEL = 1024
D_FF = 2816
DEPTH = 2
EPS = 1e-6
MLA_HEADS = 6
MLA_Q_RANK = 256
MLA_KV_RANK = 128
MLA_NOPE = 64
MLA_ROPE = 32
MLA_V = 64
ROPE_THETA = 10000.0
POOL_WINDOWS = (2, 4, 8, 16)
POOL_GROUP = 64
POOL_WIDTH = 256
FOX_HEADS = 6
FOX_HEAD_DIM = 64
HEADS = MLA_HEADS + FOX_HEADS
HEAD_PAD = 128
V_DIM = 64
LOG2E = math.log2(math.e)
SCALE_MLA = LOG2E / math.sqrt(MLA_NOPE + MLA_ROPE)
SCALE_FOX = LOG2E / math.sqrt(FOX_HEAD_DIM)
POOL_HISTORY = 16
NEG_BIG = -1e30

C_QA = 0
C_KVA = C_QA + MLA_Q_RANK
C_KR = C_KVA + MLA_KV_RANK
C_KRR = C_KR + HEAD_PAD
C_FF = C_KRR + HEAD_PAD
C_POOL = C_FF + HEAD_PAD
C_FQ = C_POOL + POOL_WIDTH
C_FK = C_FQ + FOX_HEADS * HEAD_PAD
C_FV = C_FK + FOX_HEADS * HEAD_PAD
N_IN_PAD = C_FV + FOX_HEADS * HEAD_PAD

VMEM_LIMIT = 56 * 1024 * 1024

FFN_TM = 512
FFN_FC = 1536
MIX_TS = 512
ATT_T = 512
ATT_RC = 64


def _rms(x, g):
    return x * lax.rsqrt(jnp.mean(x * x, axis=-1, keepdims=True) + EPS) * g


def _split3(x):
    hi = x.astype(BF16)
    r = x - hi.astype(F32)
    mid = r.astype(BF16)
    lo = (r - mid.astype(F32)).astype(BF16)
    return hi, mid, lo


def _const_spec(shape):
    return pl.BlockSpec(shape, lambda *_: (0,) * len(shape))


def _ffn_kernel(*refs, fc, mode, mixed):
    refs = list(refs)
    x_ref = refs.pop(0)
    if mixed:
        o_ref, yb_ref, wo_ref, wb_ref = refs[:4]
        refs = refs[4:]
    g_ref, wgu_ref, wd_ref, post_ref, out_ref = refs[:5]
    a_ref = refs[-1]
    x = x_ref[...]
    if mixed:
        x = (x + jnp.dot(o_ref[...], wo_ref[...], preferred_element_type=F32)
             + jnp.dot(yb_ref[...], wb_ref[...], preferred_element_type=F32))
    h = _rms(x, g_ref[...]).astype(BF16)
    for lo in range(0, D_FF, fc):
        hi = min(lo + fc, D_FF)
        g = jnp.dot(h, wgu_ref[:, lo:hi], preferred_element_type=F32)
        u = jnp.dot(h, wgu_ref[:, D_FF + lo:D_FF + hi], preferred_element_type=F32)
        a_ref[:, lo:hi] = (g * jax.nn.sigmoid(g) * u).astype(BF16)
    y = x + 0.5 * jnp.dot(a_ref[...], wd_ref[...], preferred_element_type=F32)
    if mode == "final":
        y = _rms(y, post_ref[...])
    out_ref[...] = y
    if mode == "next":
        refs[5][...] = _rms(y, post_ref[...]).astype(BF16)


def _resident(shape, index_map):
    return pl.BlockSpec(shape, index_map, pipeline_mode=pl.Buffered(1))


def _ffn(x, g, wgu, wd, layer, post, mode, mix=None):
    t = x.shape[0]
    tm = FFN_TM
    tile = pl.BlockSpec((tm, D_MODEL), lambda i: (i, 0))
    vec = _resident((1, D_MODEL), lambda i: (0, 0))
    operands, in_specs = [x], [tile]
    if mix is not None:
        o, yb, wo, wb = mix
        operands += [o, yb, wo, wb]
        in_specs += [pl.BlockSpec((tm, o.shape[1]), lambda i: (i, 0)),
                     pl.BlockSpec((tm, yb.shape[1]), lambda i: (i, 0)),
                     _resident(wo.shape, lambda i: (0, 0)),
                     _resident(wb.shape, lambda i: (0, 0))]
    operands += [g, wgu, wd, post]
    in_specs += [vec,
                 _resident((None, D_MODEL, 2 * D_FF), lambda i: (layer, 0, 0)),
                 _resident((None, D_FF, D_MODEL), lambda i: (layer, 0, 0)),
                 vec]
    out_specs = [tile]
    out_shape = [jax.ShapeDtypeStruct((t, D_MODEL), F32)]
    if mode == "next":
        out_specs.append(tile)
        out_shape.append(jax.ShapeDtypeStruct((t, D_MODEL), BF16))
    return pl.pallas_call(
        functools.partial(_ffn_kernel, fc=FFN_FC, mode=mode, mixed=mix is not None),
        grid=(t // tm,),
        in_specs=in_specs,
        out_specs=out_specs,
        out_shape=out_shape,
        scratch_shapes=[pltpu.VMEM((tm, D_FF), BF16)],
        compiler_params=pltpu.CompilerParams(
            dimension_semantics=("arbitrary",), vmem_limit_bytes=VMEM_LIMIT),
        name="ffn",
    )(*operands)


def _mix_kernel(h_ref, win_ref, qan_ref, wq_ref, kvn_ref, wkv_ref, cos_ref, sin_ref,
                pbd_ref, pscale_ref, fb_ref, e_ref, oq_ref, ok_ref, ov_ref,
                q_out, k_out, v_out, yb_out, hist_ref, carry_ref, *, ts):
    si = pl.program_id(1)

    @pl.when(si == 0)
    def _():
        hist_ref[...] = jnp.zeros_like(hist_ref)
        carry_ref[...] = jnp.zeros_like(carry_ref)

    h = h_ref[...]

    def proj(lo, hi):
        return jnp.dot(h, win_ref[:, lo:hi], preferred_element_type=F32)

    cos_t = cos_ref[...]
    sin_t = sin_ref[...]
    nq = MLA_HEADS * HEAD_PAD

    z_krr_ff = proj(C_KRR, C_POOL)
    z_kv_kr = proj(C_KVA, C_KRR)
    z_qa = proj(C_QA, C_KVA)
    u = proj(C_POOL, C_FQ)
    nf = FOX_HEADS * HEAD_PAD
    ones_v = ov_ref[...]
    v_out[:, nq:nq + nf] = (proj(C_FV, N_IN_PAD) + ones_v).astype(BF16)
    z_fq = proj(C_FQ, C_FK)
    z_fk = proj(C_FK, C_FV)

    hq = _rms(z_qa, qan_ref[...]).astype(BF16)
    zq = jnp.dot(hq, wq_ref[...], preferred_element_type=F32)
    for hh in range(MLA_HEADS):
        a = zq[:, hh * HEAD_PAD:(hh + 1) * HEAD_PAD]
        b = zq[:, nq + hh * HEAD_PAD:nq + (hh + 1) * HEAD_PAD]
        q_out[:, hh * HEAD_PAD:(hh + 1) * HEAD_PAD] = ((a * cos_t + b * sin_t) * SCALE_MLA).astype(BF16)

    hkv = _rms(z_kv_kr[:, :MLA_KV_RANK], kvn_ref[...]).astype(BF16)
    zkv = jnp.dot(hkv, wkv_ref[...], preferred_element_type=F32)
    kpe = z_kv_kr[:, MLA_KV_RANK:] * cos_t + z_krr_ff[:, :HEAD_PAD] * sin_t
    for hh in range(MLA_HEADS):
        k_out[:, hh * HEAD_PAD:(hh + 1) * HEAD_PAD] = (zkv[:, hh * HEAD_PAD:(hh + 1) * HEAD_PAD] + kpe).astype(BF16)
    v_out[:, 0:nq] = (zkv[:, nq:2 * nq] + ones_v).astype(BF16)

    e = jnp.concatenate([hist_ref[...], u], axis=0)
    s2 = e + pltpu.roll(e, 1, 0)
    s4 = s2 + pltpu.roll(s2, 2, 0)
    s8 = s4 + pltpu.roll(s4, 4, 0)
    s16 = s8 + pltpu.roll(s8, 8, 0)
    hist_ref[...] = u[ts - POOL_HISTORY:, :]
    lane = lax.broadcasted_iota(jnp.int32, (ts, POOL_WIDTH), 1)
    row = lax.broadcasted_iota(jnp.int32, (ts, POOL_WIDTH), 0)
    g_id = lane // POOL_GROUP
    wsum = jnp.where(g_id == 0, s2[POOL_HISTORY:], jnp.where(g_id == 1, s4[POOL_HISTORY:],
                     jnp.where(g_id == 2, s8[POOL_HISTORY:], s16[POOL_HISTORY:])))
    width = jnp.where(g_id == 0, 2, jnp.where(g_id == 1, 4, jnp.where(g_id == 2, 8, 16)))
    count = jnp.minimum(si * ts + row + 1, width).astype(F32)
    pooled = (wsum / count - u).astype(BF16)
    yb = jnp.dot(pooled, pbd_ref[...], preferred_element_type=F32) * pscale_ref[...]
    yb_out[...] = yb.astype(BF16)

    zf = z_krr_ff[:, HEAD_PAD:] + fb_ref[...]
    cum = -(jnp.maximum(-zf, 0.0) + jnp.log1p(jnp.exp(-jnp.abs(zf))))
    g_row = lax.broadcasted_iota(jnp.int32, (ts, HEAD_PAD), 0)
    shift = 1
    while shift < ts:
        cum = cum + jnp.where(g_row >= shift, pltpu.roll(cum, shift, 0), 0.0)
        shift *= 2
    cum = cum + carry_ref[...]
    carry_ref[...] = cum[ts - 1:ts, :]
    g_lane = lax.broadcasted_iota(jnp.int32, (ts, HEAD_PAD), 1)
    c0 = jnp.where(g_lane < FOX_HEADS, cum * LOG2E, 0.0)
    c_rep = c0 + pltpu.roll(c0, FOX_HEADS, 1) + pltpu.roll(c0, 2 * FOX_HEADS, 1)
    c_hi, c_mid, c_lo = _split3(c_rep)
    c_parts = jnp.where(g_lane < FOX_HEADS, c_hi, jnp.where(g_lane < 2 * FOX_HEADS, c_mid, c_lo))
    ce = jnp.dot(c_parts, e_ref[...], preferred_element_type=F32)
    q_out[:, nq:nq + nf] = (z_fq * SCALE_FOX + ce[:, :nf] + oq_ref[...]).astype(BF16)
    k_out[:, nq:nq + nf] = (z_fk + ce[:, nf:] + ok_ref[...]).astype(BF16)


def _mix(h, win, qan, wq, kvn, wkv, cos_t, sin_t, pbd, pscale, fb, emat, oq, ok, ov, batch, seq):
    t = h.shape[0]
    ts = MIX_TS
    ns = seq // ts
    row = lambda b, s: (b * ns + s, 0)
    tab = lambda b, s: (s, 0)
    return pl.pallas_call(
        functools.partial(_mix_kernel, ts=ts),
        grid=(batch, ns),
        in_specs=[
            pl.BlockSpec((ts, D_MODEL), row),
            _const_spec((D_MODEL, N_IN_PAD)),
            _const_spec((1, MLA_Q_RANK)),
            _const_spec(wq.shape),
            _const_spec((1, MLA_KV_RANK)),
            _const_spec(wkv.shape),
            pl.BlockSpec((ts, HEAD_PAD), tab),
            pl.BlockSpec((ts, HEAD_PAD), tab),
            _const_spec(pbd.shape),
            _const_spec((1, POOL_WIDTH)),
            _const_spec((1, HEAD_PAD)),
            _const_spec(emat.shape),
            _const_spec(oq.shape),
            _const_spec(ok.shape),
            _const_spec(ov.shape),
        ],
        out_specs=[
            pl.BlockSpec((ts, HEADS * HEAD_PAD), row),
            pl.BlockSpec((ts, HEADS * HEAD_PAD), row),
            pl.BlockSpec((ts, HEADS * HEAD_PAD), row),
            pl.BlockSpec((ts, POOL_WIDTH), row),
        ],
        out_shape=[
            jax.ShapeDtypeStruct((t, HEADS * HEAD_PAD), BF16),
            jax.ShapeDtypeStruct((t, HEADS * HEAD_PAD), BF16),
            jax.ShapeDtypeStruct((t, HEADS * HEAD_PAD), BF16),
            jax.ShapeDtypeStruct((t, POOL_WIDTH), BF16),
        ],
        scratch_shapes=[pltpu.VMEM((POOL_HISTORY, POOL_WIDTH), F32), pltpu.VMEM((1, HEAD_PAD), F32)],
        compiler_params=pltpu.CompilerParams(
            dimension_semantics=("arbitrary", "arbitrary"), vmem_limit_bytes=VMEM_LIMIT),
        name="mixproj",
    )(h, win, qan, wq, kvn, wkv, cos_t, sin_t, pbd, pscale, fb, emat, oq, ok, ov)


def _attn_kernel(q_ref, k_ref, v_ref, o_ref, s_ref, p_ref, m_ref, al_ref, acc_ref, *, t, rc, nq):
    nkb = t // HEAD_PAD
    head_cols = [slice(hh * HEAD_PAD, (hh + 1) * HEAD_PAD) for hh in range(2)]

    def rows_of(blk):
        return pl.ds(pl.multiple_of(blk * t, t), t)

    half = t // 2

    def scores(i, j, hh, diag):
        dims = (((1,), (1,)), ((), ()))
        if diag:
            q_top = q_ref[pl.ds(pl.multiple_of(i * t, t), half), head_cols[hh]]
            q_bot = q_ref[pl.ds(pl.multiple_of(i * t + half, half), half), head_cols[hh]]
            k_top = k_ref[pl.ds(pl.multiple_of(j * t, t), half), head_cols[hh]]
            s_ref[hh, 0:half, 0:half] = lax.dot_general(q_top, k_top, dims, preferred_element_type=F32)
            s_ref[hh, half:t, :] = lax.dot_general(q_bot, k_ref[rows_of(j), head_cols[hh]], dims,
                                                   preferred_element_type=F32)
        else:
            s_ref[hh] = lax.dot_general(q_ref[rows_of(i), head_cols[hh]], k_ref[rows_of(j), head_cols[hh]],
                                        dims, preferred_element_type=F32)

    def softmax(hh, masked, first):
        for c in range(t // rc):
            r0 = c * rc
            rows = slice(r0, r0 + rc)
            live = (r0 + rc - 1) // HEAD_PAD + 1 if masked else nkb
            blocks = []
            for kb in range(live):
                blk = s_ref[hh, rows, kb * HEAD_PAD:(kb + 1) * HEAD_PAD]
                if masked and (kb + 1) * HEAD_PAD - 1 > r0:
                    r_i = r0 + lax.broadcasted_iota(jnp.int32, (rc, HEAD_PAD), 0)
                    c_i = kb * HEAD_PAD + lax.broadcasted_iota(jnp.int32, (rc, HEAD_PAD), 1)
                    blk = jnp.where(c_i <= r_i, blk, NEG_BIG)
                blocks.append(blk)
            bmax = blocks[0]
            for blk in blocks[1:]:
                bmax = jnp.maximum(bmax, blk)
            rmax = jnp.max(bmax, axis=-1, keepdims=True)
            m_old = jnp.full((rc, HEAD_PAD), NEG_BIG, F32) if first else m_ref[hh, rows, :]
            m_new = jnp.maximum(m_old, rmax)
            al_ref[hh, rows, :] = jnp.exp2(m_old - m_new)
            m_ref[hh, rows, :] = m_new
            for kb in range(nkb):
                pcols = slice(kb * HEAD_PAD, (kb + 1) * HEAD_PAD)
                if kb < live:
                    p_ref[hh, rows, pcols] = jnp.exp2(blocks[kb] - m_new).astype(BF16)
                elif r0 >= half or (kb + 1) * HEAD_PAD <= half:
                    p_ref[hh, rows, pcols] = (blocks[0] * 0.0).astype(BF16)

    def accumulate(j, hh, diag):
        if diag:
            v_top = v_ref[pl.ds(pl.multiple_of(j * t, t), half), head_cols[hh]]
            pv_top = jnp.dot(p_ref[hh, 0:half, 0:half], v_top, preferred_element_type=F32)
            acc_ref[hh, 0:half, :] = al_ref[hh, 0:half, :] * acc_ref[hh, 0:half, :] + pv_top
            pv_bot = jnp.dot(p_ref[hh, half:t, :], v_ref[rows_of(j), head_cols[hh]], preferred_element_type=F32)
            acc_ref[hh, half:t, :] = al_ref[hh, half:t, :] * acc_ref[hh, half:t, :] + pv_bot
        else:
            pv = jnp.dot(p_ref[hh], v_ref[rows_of(j), head_cols[hh]], preferred_element_type=F32)
            acc_ref[hh] = al_ref[hh] * acc_ref[hh] + pv

    def finish(i):
        outs = []
        for hh in range(2):
            acc = acc_ref[hh]
            outs.append(acc[:, :V_DIM] / acc[:, V_DIM:V_DIM + 1])
        o_ref[rows_of(i), :] = jnp.concatenate(outs, axis=1).astype(BF16)

    for hh in range(2):
        zero = v_ref[0:t, head_cols[hh]].astype(F32) * 0.0
        acc_ref[hh] = zero
        if hh == 1:
            m_ref[hh] = zero + NEG_BIG
    scores(0, 0, 0, True)
    scores(0, 0, 1, True)
    softmax(0, True, True)

    def stage(i, j, ni, nj, cur_diag, next_diag):
        accumulate(j, 0, cur_diag)
        scores(ni, nj, 0, next_diag)
        softmax(1, cur_diag, False)
        scores(ni, nj, 1, next_diag)
        softmax(0, next_diag, cur_diag)
        accumulate(j, 1, cur_diag)
        if cur_diag:
            finish(i)
            m_ref[1] = m_ref[1] * 0.0 + NEG_BIG

    def body(n, carry):
        i, j = carry

        @pl.when(j == i)
        def _():
            stage(i, j, i + 1, 0, True, False)

        @pl.when(j + 1 == i)
        def _():
            stage(i, j, i, j + 1, False, True)

        @pl.when(j + 2 == i)
        def _():
            stage(i, j, i, j + 1, False, False)

        @pl.when(j + 2 < i)
        def _():
            stage(i, j, i, j + 1, False, False)
            stage(i, j + 1, i, j + 2, False, False)

        ni = jnp.where(j == i, i + 1, i)
        nj = jnp.where(j == i, 0, jnp.where(j + 2 < i, j + 2, j + 1))
        return ni, nj

    n_iter, wi, wj = 0, 0, 0
    while (wi, wj) != (nq - 1, nq - 1):
        wi, wj = (wi + 1, 0) if wj == wi else (wi, wj + 2 if wj + 2 < wi else wj + 1)
        n_iter += 1
    last_i, last_j = lax.fori_loop(0, n_iter, body, (jnp.int32(0), jnp.int32(0)))
    softmax(1, True, False)
    accumulate(last_j, 0, True)
    accumulate(last_j, 1, True)
    finish(last_i)


def _attention(q, k, v, batch, seq):
    t = ATT_T
    nq = seq // t
    pairs = HEADS // 2
    whole = lambda b, p: (b, p)
    return pl.pallas_call(
        functools.partial(_attn_kernel, t=t, rc=ATT_RC, nq=nq),
        grid=(batch, pairs),
        in_specs=[
            pl.BlockSpec((seq, 2 * HEAD_PAD), whole),
            pl.BlockSpec((seq, 2 * HEAD_PAD), whole),
            pl.BlockSpec((seq, 2 * HEAD_PAD), whole),
        ],
        out_specs=pl.BlockSpec((seq, 2 * V_DIM), whole),
        out_shape=jax.ShapeDtypeStruct((batch * seq, HEADS * V_DIM), BF16),
        scratch_shapes=[
            pltpu.VMEM((2, t, t), F32),
            pltpu.VMEM((2, t, t), BF16),
            pltpu.VMEM((2, t, HEAD_PAD), F32),
            pltpu.VMEM((2, t, HEAD_PAD), F32),
            pltpu.VMEM((2, t, HEAD_PAD), F32),
        ],
        compiler_params=pltpu.CompilerParams(
            dimension_semantics=("arbitrary", "arbitrary"), vmem_limit_bytes=VMEM_LIMIT),
        name="attention",
    )(q, k, v)


def _zeros(rows, cols):
    return jnp.zeros((rows, cols), BF16)


def _rot_pieces(w):
    half = w.shape[-1] // 2
    return [-w[:, half:], w[:, :half]]


def _prep_mix_weights(w_in, w_q_b, w_kv_b, pool_w, w_out):
    o2 = MLA_Q_RANK + MLA_KV_RANK
    o3 = o2 + MLA_ROPE
    o4 = o3 + POOL_WIDTH
    hd = FOX_HEADS * FOX_HEAD_DIM
    o5 = o4 + 3 * hd
    d = w_in.shape[0]
    rope_tail = HEAD_PAD - MLA_NOPE - MLA_ROPE
    w_kr = w_in[:, o2:o3]
    small = jnp.concatenate([
        _zeros(d, MLA_NOPE), w_kr, _zeros(d, rope_tail),
        _zeros(d, MLA_NOPE), *_rot_pieces(w_kr), _zeros(d, rope_tail),
        w_in[:, o5:], _zeros(d, HEAD_PAD - FOX_HEADS)], axis=1)
    fox = jnp.pad(w_in[:, o4:o5].reshape(d, 3 * FOX_HEADS, FOX_HEAD_DIM),
                  ((0, 0), (0, 0), (0, HEAD_PAD - FOX_HEAD_DIM))).reshape(d, 3 * FOX_HEADS * HEAD_PAD)
    win = jnp.concatenate([w_in[:, 0:o2], small, w_in[:, o3:o4], fox], axis=1)
    assert win.shape == (d, N_IN_PAD)

    qd = MLA_NOPE + MLA_ROPE
    plain, rot = [], []
    for hh in range(MLA_HEADS):
        plain += [w_q_b[:, hh * qd:(hh + 1) * qd], _zeros(MLA_Q_RANK, rope_tail)]
        rot += [_zeros(MLA_Q_RANK, MLA_NOPE), *_rot_pieces(w_q_b[:, hh * qd + MLA_NOPE:(hh + 1) * qd]),
                _zeros(MLA_Q_RANK, rope_tail)]
    wq = jnp.concatenate(plain + rot, axis=1)

    kd = MLA_NOPE + MLA_V
    keys, vals = [], []
    for hh in range(MLA_HEADS):
        keys += [w_kv_b[:, hh * kd:hh * kd + MLA_NOPE], _zeros(MLA_KV_RANK, HEAD_PAD - MLA_NOPE)]
        vals += [w_kv_b[:, hh * kd + MLA_NOPE:(hh + 1) * kd], _zeros(MLA_KV_RANK, HEAD_PAD - MLA_V)]
    wkv = jnp.concatenate(keys + vals, axis=1)

    ng = len(POOL_WINDOWS)
    pbd = jnp.concatenate([
        jnp.concatenate([pool_w[g] if c == g else _zeros(POOL_GROUP, POOL_GROUP) for c in range(ng)], axis=1)
        for g in range(ng)], axis=0)

    na = MLA_HEADS * MLA_V
    wo = jnp.concatenate([w_out[:na], w_out[na + POOL_WIDTH:]], axis=0)
    wb = w_out[na:na + POOL_WIDTH]
    return win, wq, wkv, pbd, wo, wb


def _gate_lane_constants():
    nf = FOX_HEADS * HEAD_PAD
    e = [[0.0] * (2 * nf) for _ in range(HEAD_PAD)]
    oq = [0.0] * nf
    ok = [0.0] * nf
    ov = [0.0] * nf
    for hh in range(FOX_HEADS):
        ov[hh * HEAD_PAD + V_DIM] = 1.0
        for part in range(3):
            e[part * FOX_HEADS + hh][hh * HEAD_PAD + FOX_HEAD_DIM + part] = 1.0
            e[part * FOX_HEADS + hh][nf + hh * HEAD_PAD + FOX_HEAD_DIM + 3 + part] = -1.0
            oq[hh * HEAD_PAD + FOX_HEAD_DIM + 3 + part] = 1.0
            ok[hh * HEAD_PAD + FOX_HEAD_DIM + part] = 1.0
    row = lambda r: jnp.array(r, F32).reshape(1, nf)
    return jnp.array(e, F32).astype(BF16), row(oq), row(ok), row(ov)


def _rope_tables(seq):
    r = MLA_ROPE
    inv_freq = ROPE_THETA ** (-np.arange(0, r, 2, dtype=np.float64) / r)
    ang = np.arange(seq, dtype=np.float64)[:, None] * inv_freq[None, :]
    cos = np.cos(ang).astype(np.float32)
    sin = np.sin(ang).astype(np.float32)
    tail = HEAD_PAD - MLA_NOPE - MLA_ROPE
    cos_t = np.concatenate([np.ones((seq, MLA_NOPE), np.float32), cos, cos, np.zeros((seq, tail), np.float32)], axis=1)
    sin_t = np.concatenate([np.zeros((seq, MLA_NOPE), np.float32), sin, sin, np.zeros((seq, tail), np.float32)], axis=1)
    return jnp.asarray(cos_t), jnp.asarray(sin_t)


def kernel(x, ffn1_norm, ffn1_w_gu, ffn1_w_down, mix_norm, w_in, q_a_norm, w_q_b, kv_a_norm, w_kv_b, pool_w, pool_scale, fox_b_f, w_out, ffn2_norm, ffn2_w_gu, ffn2_w_down, final_norm):
    batch, seq, d = x.shape
    xf = x.reshape(batch * seq, d)
    cos_t, sin_t = _rope_tables(seq)
    emat, oq, ok, ov = _gate_lane_constants()
    fin = final_norm.reshape(1, d)
    wgu1, wd1 = ffn1_w_gu.astype(BF16), ffn1_w_down.astype(BF16)
    wgu2, wd2 = ffn2_w_gu.astype(BF16), ffn2_w_down.astype(BF16)
    w_in_b, w_q_b_b, w_kv_b_b = w_in.astype(BF16), w_q_b.astype(BF16), w_kv_b.astype(BF16)
    pool_w_b, w_out_b = pool_w.astype(BF16), w_out.astype(BF16)
    for l in range(DEPTH):
        xf, hmix = _ffn(xf, ffn1_norm[l].reshape(1, d), wgu1, wd1, l, mix_norm[l].reshape(1, d), "next")
        win, wq, wkv, pbd, wo, wb = _prep_mix_weights(w_in_b[l], w_q_b_b[l], w_kv_b_b[l], pool_w_b[l], w_out_b[l])
        fb = jnp.pad(fox_b_f[l], (0, HEAD_PAD - FOX_HEADS)).reshape(1, HEAD_PAD)
        q, k, v, yb = _mix(hmix, win, q_a_norm[l].reshape(1, -1), wq, kv_a_norm[l].reshape(1, -1), wkv,
                           cos_t, sin_t, pbd, pool_scale[l].reshape(1, -1), fb, emat, oq, ok, ov, batch, seq)
        o = _attention(q, k, v, batch, seq)
        last = l == DEPTH - 1
        xf = _ffn(xf, ffn2_norm[l].reshape(1, d), wgu2, wd2, l, fin, "final" if last else "plain",
                  mix=(o, yb, wo, wb))[0]
    return xf.reshape(batch, seq, d)
```

```python
import functools
import math

import jax
import jax.numpy as jnp
import numpy as np
from jax import lax
from jax.experimental import pallas as pl
from jax.experimental.pallas import tpu as pltpu

F32 = jnp.float32
BF16 = jnp.bfloat16

D_MODEL = 1024
D_FF = 2816
DEPTH = 2
EPS = 1e-6
MLA_HEADS = 6
MLA_Q_RANK = 256
MLA_KV_RANK = 128
MLA_NOPE = 64
MLA_ROPE = 32
MLA_V = 64
ROPE_THETA = 10000.0
POOL_WINDOWS = (2, 4, 8, 16)
POOL_GROUP = 64
POOL_WIDTH = 256
FOX_HEADS = 6
FOX_HEAD_DIM = 64
HEADS = MLA_HEADS + FOX_HEADS
HEAD_PAD = 128
V_DIM = 64
LOG2E = math.log2(math.e)
SCALE_MLA = LOG2E / math.sqrt(MLA_NOPE + MLA_ROPE)
SCALE_FOX = LOG2E / math.sqrt(FOX_HEAD_DIM)
POOL_HISTORY = 16
NEG_BIG = -1e30

C_QA = 0
C_KVA = C_QA + MLA_Q_RANK
C_KR = C_KVA + MLA_KV_RANK
C_KRR = C_KR + HEAD_PAD
C_FF = C_KRR + HEAD_PAD
C_POOL = C_FF + HEAD_PAD
C_FQ = C_POOL + POOL_WIDTH
C_FK = C_FQ + FOX_HEADS * HEAD_PAD
C_FV = C_FK + FOX_HEADS * HEAD_PAD
N_IN_PAD = C_FV + FOX_HEADS * HEAD_PAD

VMEM_LIMIT = 56 * 1024 * 1024

FFN_TM = 512
FFN_FC = 1536
MIX_TS = 512
ATT_T = 512
ATT_RC = 64
ATT_PAIRS = 2


def _rms(x, g):
    return x * lax.rsqrt(jnp.mean(x * x, axis=-1, keepdims=True) + EPS) * g


def _split3(x):
    hi = x.astype(BF16)
    r = x - hi.astype(F32)
    mid = r.astype(BF16)
    lo = (r - mid.astype(F32)).astype(BF16)
    return hi, mid, lo


def _const_spec(shape):
    return pl.BlockSpec(shape, lambda *_: (0,) * len(shape))


def _ffn_kernel(*refs, fc, mode, mixed):
    refs = list(refs)
    x_ref = refs.pop(0)
    if mixed:
        o_ref, yb_ref, wo_ref, wb_ref = refs[:4]
        refs = refs[4:]
    g_ref, wgu_ref, wd_ref, post_ref, out_ref = refs[:5]
    a_ref = refs[-1]
    x = x_ref[...]
    if mixed:
        x = (x + jnp.dot(o_ref[...], wo_ref[...], preferred_element_type=F32)
             + jnp.dot(yb_ref[...], wb_ref[...], preferred_element_type=F32))
    h = _rms(x, g_ref[...]).astype(BF16)
    for lo in range(0, D_FF, fc):
        hi = min(lo + fc, D_FF)
        g = jnp.dot(h, wgu_ref[:, lo:hi], preferred_element_type=F32)
        u = jnp.dot(h, wgu_ref[:, D_FF + lo:D_FF + hi], preferred_element_type=F32)
        a_ref[:, lo:hi] = (g * jax.nn.sigmoid(g) * u).astype(BF16)
    y = x + 0.5 * jnp.dot(a_ref[...], wd_ref[...], preferred_element_type=F32)
    if mode == "final":
        y = _rms(y, post_ref[...])
    out_ref[...] = y
    if mode == "next":
        refs[5][...] = _rms(y, post_ref[...]).astype(BF16)


def _resident(shape, index_map):
    return pl.BlockSpec(shape, index_map, pipeline_mode=pl.Buffered(1))


def _ffn(x, g, wgu, wd, layer, post, mode, mix=None):
    t = x.shape[0]
    tm = FFN_TM
    tile = pl.BlockSpec((tm, D_MODEL), lambda i: (i, 0))
    vec = _resident((1, D_MODEL), lambda i: (0, 0))
    operands, in_specs = [x], [tile]
    if mix is not None:
        o, yb, wo, wb = mix
        operands += [o, yb, wo, wb]
        in_specs += [pl.BlockSpec((tm, o.shape[1]), lambda i: (i, 0)),
                     pl.BlockSpec((tm, yb.shape[1]), lambda i: (i, 0)),
                     _resident(wo.shape, lambda i: (0, 0)),
                     _resident(wb.shape, lambda i: (0, 0))]
    operands += [g, wgu, wd, post]
    in_specs += [vec,
                 _resident((None, D_MODEL, 2 * D_FF), lambda i: (layer, 0, 0)),
                 _resident((None, D_FF, D_MODEL), lambda i: (layer, 0, 0)),
                 vec]
    out_specs = [tile]
    out_shape = [jax.ShapeDtypeStruct((t, D_MODEL), F32)]
    if mode == "next":
        out_specs.append(tile)
        out_shape.append(jax.ShapeDtypeStruct((t, D_MODEL), BF16))
    return pl.pallas_call(
        functools.partial(_ffn_kernel, fc=FFN_FC, mode=mode, mixed=mix is not None),
        grid=(t // tm,),
        in_specs=in_specs,
        out_specs=out_specs,
        out_shape=out_shape,
        scratch_shapes=[pltpu.VMEM((tm, D_FF), BF16)],
        compiler_params=pltpu.CompilerParams(
            dimension_semantics=("arbitrary",), vmem_limit_bytes=VMEM_LIMIT),
        name="ffn",
    )(*operands)


def _mix_kernel(h_ref, win_ref, qan_ref, wq_ref, kvn_ref, wkv_ref, cos_ref, sin_ref,
                pbd_ref, pscale_ref, fb_ref, e_ref, oq_ref, ok_ref, ov_ref,
                q_out, k_out, v_out, yb_out, hist_ref, carry_ref, *, ts):
    si = pl.program_id(1)

    @pl.when(si == 0)
    def _():
        hist_ref[...] = jnp.zeros_like(hist_ref)
        carry_ref[...] = jnp.zeros_like(carry_ref)

    h = h_ref[...]

    def proj(lo, hi):
        return jnp.dot(h, win_ref[:, lo:hi], preferred_element_type=F32)

    cos_t = cos_ref[...]
    sin_t = sin_ref[...]
    nq = MLA_HEADS * HEAD_PAD

    z_krr_ff = proj(C_KRR, C_POOL)
    z_kv_kr = proj(C_KVA, C_KRR)
    z_qa = proj(C_QA, C_KVA)
    u = proj(C_POOL, C_FQ)
    nf = FOX_HEADS * HEAD_PAD
    ones_v = ov_ref[...]
    v_out[:, nq:nq + nf] = (proj(C_FV, N_IN_PAD) + ones_v).astype(BF16)
    z_fq = proj(C_FQ, C_FK)
    z_fk = proj(C_FK, C_FV)

    hq = _rms(z_qa, qan_ref[...]).astype(BF16)
    zq = jnp.dot(hq, wq_ref[...], preferred_element_type=F32)
    for hh in range(MLA_HEADS):
        a = zq[:, hh * HEAD_PAD:(hh + 1) * HEAD_PAD]
        b = zq[:, nq + hh * HEAD_PAD:nq + (hh + 1) * HEAD_PAD]
        q_out[:, hh * HEAD_PAD:(hh + 1) * HEAD_PAD] = ((a * cos_t + b * sin_t) * SCALE_MLA).astype(BF16)

    hkv = _rms(z_kv_kr[:, :MLA_KV_RANK], kvn_ref[...]).astype(BF16)
    zkv = jnp.dot(hkv, wkv_ref[...], preferred_element_type=F32)
    kpe = z_kv_kr[:, MLA_KV_RANK:] * cos_t + z_krr_ff[:, :HEAD_PAD] * sin_t
    for hh in range(MLA_HEADS):
        k_out[:, hh * HEAD_PAD:(hh + 1) * HEAD_PAD] = (zkv[:, hh * HEAD_PAD:(hh + 1) * HEAD_PAD] + kpe).astype(BF16)
    v_out[:, 0:nq] = (zkv[:, nq:2 * nq] + ones_v).astype(BF16)

    e = jnp.concatenate([hist_ref[...], u], axis=0)
    s2 = e + pltpu.roll(e, 1, 0)
    s4 = s2 + pltpu.roll(s2, 2, 0)
    s8 = s4 + pltpu.roll(s4, 4, 0)
    s16 = s8 + pltpu.roll(s8, 8, 0)
    hist_ref[...] = u[ts - POOL_HISTORY:, :]
    lane = lax.broadcasted_iota(jnp.int32, (ts, POOL_WIDTH), 1)
    row = lax.broadcasted_iota(jnp.int32, (ts, POOL_WIDTH), 0)
    g_id = lane // POOL_GROUP
    wsum = jnp.where(g_id == 0, s2[POOL_HISTORY:], jnp.where(g_id == 1, s4[POOL_HISTORY:],
                     jnp.where(g_id == 2, s8[POOL_HISTORY:], s16[POOL_HISTORY:])))
    width = jnp.where(g_id == 0, 2, jnp.where(g_id == 1, 4, jnp.where(g_id == 2, 8, 16)))
    count = jnp.minimum(si * ts + row + 1, width).astype(F32)
    pooled = (wsum / count - u).astype(BF16)
    yb = jnp.dot(pooled, pbd_ref[...], preferred_element_type=F32) * pscale_ref[...]
    yb_out[...] = yb.astype(BF16)

    zf = z_krr_ff[:, HEAD_PAD:] + fb_ref[...]
    cum = -(jnp.maximum(-zf, 0.0) + jnp.log1p(jnp.exp(-jnp.abs(zf))))
    g_row = lax.broadcasted_iota(jnp.int32, (ts, HEAD_PAD), 0)
    shift = 1
    while shift < ts:
        cum = cum + jnp.where(g_row >= shift, pltpu.roll(cum, shift, 0), 0.0)
        shift *= 2
    cum = cum + carry_ref[...]
    carry_ref[...] = cum[ts - 1:ts, :]
    g_lane = lax.broadcasted_iota(jnp.int32, (ts, HEAD_PAD), 1)
    c0 = jnp.where(g_lane < FOX_HEADS, cum * LOG2E, 0.0)
    c_rep = c0 + pltpu.roll(c0, FOX_HEADS, 1) + pltpu.roll(c0, 2 * FOX_HEADS, 1)
    c_hi, c_mid, c_lo = _split3(c_rep)
    c_parts = jnp.where(g_lane < FOX_HEADS, c_hi, jnp.where(g_lane < 2 * FOX_HEADS, c_mid, c_lo))
    ce = jnp.dot(c_parts, e_ref[...], preferred_element_type=F32)
    q_out[:, nq:nq + nf] = (z_fq * SCALE_FOX + ce[:, :nf] + oq_ref[...]).astype(BF16)
    k_out[:, nq:nq + nf] = (z_fk + ce[:, nf:] + ok_ref[...]).astype(BF16)


def _mix(h, win, qan, wq, kvn, wkv, cos_t, sin_t, pbd, pscale, fb, emat, oq, ok, ov, batch, seq):
    t = h.shape[0]
    ts = MIX_TS
    ns = seq // ts
    row = lambda b, s: (b * ns + s, 0)
    tab = lambda b, s: (s, 0)
    return pl.pallas_call(
        functools.partial(_mix_kernel, ts=ts),
        grid=(batch, ns),
        in_specs=[
            pl.BlockSpec((ts, D_MODEL), row),
            _const_spec((D_MODEL, N_IN_PAD)),
            _const_spec((1, MLA_Q_RANK)),
            _const_spec(wq.shape),
            _const_spec((1, MLA_KV_RANK)),
            _const_spec(wkv.shape),
            pl.BlockSpec((ts, HEAD_PAD), tab),
            pl.BlockSpec((ts, HEAD_PAD), tab),
            _const_spec(pbd.shape),
            _const_spec((1, POOL_WIDTH)),
            _const_spec((1, HEAD_PAD)),
            _const_spec(emat.shape),
            _const_spec(oq.shape),
            _const_spec(ok.shape),
            _const_spec(ov.shape),
        ],
        out_specs=[
            pl.BlockSpec((ts, HEADS * HEAD_PAD), row),
            pl.BlockSpec((ts, HEADS * HEAD_PAD), row),
            pl.BlockSpec((ts, HEADS * HEAD_PAD), row),
            pl.BlockSpec((ts, POOL_WIDTH), row),
        ],
        out_shape=[
            jax.ShapeDtypeStruct((t, HEADS * HEAD_PAD), BF16),
            jax.ShapeDtypeStruct((t, HEADS * HEAD_PAD), BF16),
            jax.ShapeDtypeStruct((t, HEADS * HEAD_PAD), BF16),
            jax.ShapeDtypeStruct((t, POOL_WIDTH), BF16),
        ],
        scratch_shapes=[pltpu.VMEM((POOL_HISTORY, POOL_WIDTH), F32), pltpu.VMEM((1, HEAD_PAD), F32)],
        compiler_params=pltpu.CompilerParams(
            dimension_semantics=("arbitrary", "arbitrary"), vmem_limit_bytes=VMEM_LIMIT),
        name="mixproj",
    )(h, win, qan, wq, kvn, wkv, cos_t, sin_t, pbd, pscale, fb, emat, oq, ok, ov)


def _attn_kernel(q_ref, k_ref, v_ref, o_ref, s_ref, p_ref, m_ref, al_ref, acc_ref, *, t, rc, nq, npair):
    nkb = t // HEAD_PAD
    nh = 2 * npair
    head_cols = [slice(hh * HEAD_PAD, (hh + 1) * HEAD_PAD) for hh in range(nh)]

    def rows_of(blk):
        return pl.ds(pl.multiple_of(blk * t, t), t)

    half = t // 2

    def scores(i, j, hh, diag):
        dims = (((1,), (1,)), ((), ()))
        if diag:
            q_top = q_ref[pl.ds(pl.multiple_of(i * t, t), half), head_cols[hh]]
            q_bot = q_ref[pl.ds(pl.multiple_of(i * t + half, half), half), head_cols[hh]]
            k_top = k_ref[pl.ds(pl.multiple_of(j * t, t), half), head_cols[hh]]
            s_ref[hh, 0:half, 0:half] = lax.dot_general(q_top, k_top, dims, preferred_element_type=F32)
            s_ref[hh, half:t, :] = lax.dot_general(q_bot, k_ref[rows_of(j), head_cols[hh]], dims,
                                                   preferred_element_type=F32)
        else:
            s_ref[hh] = lax.dot_general(q_ref[rows_of(i), head_cols[hh]], k_ref[rows_of(j), head_cols[hh]],
                                        dims, preferred_element_type=F32)

    def softmax(hh, masked, first):
        for c in range(t // rc):
            r0 = c * rc
            rows = slice(r0, r0 + rc)
            live = (r0 + rc - 1) // HEAD_PAD + 1 if masked else nkb
            blocks = []
            for kb in range(live):
                blk = s_ref[hh, rows, kb * HEAD_PAD:(kb + 1) * HEAD_PAD]
                if masked and (kb + 1) * HEAD_PAD - 1 > r0:
                    r_i = r0 + lax.broadcasted_iota(jnp.int32, (rc, HEAD_PAD), 0)
                    c_i = kb * HEAD_PAD + lax.broadcasted_iota(jnp.int32, (rc, HEAD_PAD), 1)
                    blk = jnp.where(c_i <= r_i, blk, NEG_BIG)
                blocks.append(blk)
            bmax = blocks[0]
            for blk in blocks[1:]:
                bmax = jnp.maximum(bmax, blk)
            rmax = jnp.max(bmax, axis=-1, keepdims=True)
            m_old = jnp.full((rc, HEAD_PAD), NEG_BIG, F32) if first else m_ref[hh, rows, :]
            m_new = jnp.maximum(m_old, rmax)
            al_ref[hh, rows, :] = jnp.exp2(m_old - m_new)
            m_ref[hh, rows, :] = m_new
            for kb in range(nkb):
                pcols = slice(kb * HEAD_PAD, (kb + 1) * HEAD_PAD)
                if kb < live:
                    p_ref[hh, rows, pcols] = jnp.exp2(blocks[kb] - m_new).astype(BF16)
                elif r0 >= half or (kb + 1) * HEAD_PAD <= half:
                    p_ref[hh, rows, pcols] = (blocks[0] * 0.0).astype(BF16)

    def accumulate(j, hh, diag):
        if diag:
            v_top = v_ref[pl.ds(pl.multiple_of(j * t, t), half), head_cols[hh]]
            pv_top = jnp.dot(p_ref[hh, 0:half, 0:half], v_top, preferred_element_type=F32)
            acc_ref[hh, 0:half, :] = al_ref[hh, 0:half, :] * acc_ref[hh, 0:half, :] + pv_top
            pv_bot = jnp.dot(p_ref[hh, half:t, :], v_ref[rows_of(j), head_cols[hh]], preferred_element_type=F32)
            acc_ref[hh, half:t, :] = al_ref[hh, half:t, :] * acc_ref[hh, half:t, :] + pv_bot
        else:
            pv = jnp.dot(p_ref[hh], v_ref[rows_of(j), head_cols[hh]], preferred_element_type=F32)
            acc_ref[hh] = al_ref[hh] * acc_ref[hh] + pv

    def finish(i):
        outs = []
        for hh in range(nh):
            acc = acc_ref[hh]
            outs.append(acc[:, :V_DIM] / acc[:, V_DIM:V_DIM + 1])
        o_ref[rows_of(i), :] = jnp.concatenate(outs, axis=1).astype(BF16)

    for hh in range(nh):
        zero = v_ref[0:t, head_cols[hh]].astype(F32) * 0.0
        acc_ref[hh] = zero
        if hh % 2 == 1:
            m_ref[hh] = zero + NEG_BIG
    for ha in range(0, nh, 2):
        scores(0, 0, ha, True)
        scores(0, 0, ha + 1, True)
        softmax(ha, True, True)

    def stage(i, j, ni, nj, cur_diag, next_diag):
        for ha in range(0, nh, 2):
            hb = ha + 1
            accumulate(j, ha, cur_diag)
            scores(ni, nj, ha, next_diag)
            softmax(hb, cur_diag, False)
            scores(ni, nj, hb, next_diag)
            softmax(ha, next_diag, cur_diag)
            accumulate(j, hb, cur_diag)
        if cur_diag:
            finish(i)
            for hb in range(1, nh, 2):
                m_ref[hb] = m_ref[hb] * 0.0 + NEG_BIG

    def body(n, carry):
        i, j = carry

        @pl.when(j == i)
        def _():
            stage(i, j, i + 1, 0, True, False)

        @pl.when(j + 1 == i)
        def _():
            stage(i, j, i, j + 1, False, True)

        @pl.when(j + 2 == i)
        def _():
            stage(i, j, i, j + 1, False, False)

        @pl.when(j + 2 < i)
        def _():
            stage(i, j, i, j + 1, False, False)
            stage(i, j + 1, i, j + 2, False, False)

        ni = jnp.where(j == i, i + 1, i)
        nj = jnp.where(j == i, 0, jnp.where(j + 2 < i, j + 2, j + 1))
        return ni, nj

    n_iter, wi, wj = 0, 0, 0
    while (wi, wj) != (nq - 1, nq - 1):
        wi, wj = (wi + 1, 0) if wj == wi else (wi, wj + 2 if wj + 2 < wi else wj + 1)
        n_iter += 1
    last_i, last_j = lax.fori_loop(0, n_iter, body, (jnp.int32(0), jnp.int32(0)))
    for ha in range(0, nh, 2):
        softmax(ha + 1, True, False)
        accumulate(last_j, ha, True)
        accumulate(last_j, ha + 1, True)
    finish(last_i)


def _attention(q, k, v, batch, seq):
    t = ATT_T
    nq = seq // t
    nh = 2 * ATT_PAIRS
    whole = lambda b, p: (b, p)
    return pl.pallas_call(
        functools.partial(_attn_kernel, t=t, rc=ATT_RC, nq=nq, npair=ATT_PAIRS),
        grid=(batch, HEADS // nh),
        in_specs=[
            pl.BlockSpec((seq, nh * HEAD_PAD), whole),
            pl.BlockSpec((seq, nh * HEAD_PAD), whole),
            pl.BlockSpec((seq, nh * HEAD_PAD), whole),
        ],
        out_specs=pl.BlockSpec((seq, nh * V_DIM), whole),
        out_shape=jax.ShapeDtypeStruct((batch * seq, HEADS * V_DIM), BF16),
        scratch_shapes=[
            pltpu.VMEM((nh, t, t), F32),
            pltpu.VMEM((nh, t, t), BF16),
            pltpu.VMEM((nh, t, HEAD_PAD), F32),
            pltpu.VMEM((nh, t, HEAD_PAD), F32),
            pltpu.VMEM((nh, t, HEAD_PAD), F32),
        ],
        compiler_params=pltpu.CompilerParams(
            dimension_semantics=("arbitrary", "arbitrary"), vmem_limit_bytes=VMEM_LIMIT),
        name="attention",
    )(q, k, v)


def _zeros(rows, cols):
    return jnp.zeros((rows, cols), BF16)


def _rot_pieces(w):
    half = w.shape[-1] // 2
    return [-w[:, half:], w[:, :half]]


def _prep_mix_weights(w_in, w_q_b, w_kv_b, pool_w, w_out):
    o2 = MLA_Q_RANK + MLA_KV_RANK
    o3 = o2 + MLA_ROPE
    o4 = o3 + POOL_WIDTH
    hd = FOX_HEADS * FOX_HEAD_DIM
    o5 = o4 + 3 * hd
    d = w_in.shape[0]
    rope_tail = HEAD_PAD - MLA_NOPE - MLA_ROPE
    w_in = [w_in[:, lo:hi].astype(BF16) for lo, hi in ((0, o2), (o2, o3), (o3, o4), (o4, o5), (o5, None))]
    w_lat, w_kr, w_pool, w_fox, w_gate = w_in
    small = jnp.concatenate([
        _zeros(d, MLA_NOPE), w_kr, _zeros(d, rope_tail),
        _zeros(d, MLA_NOPE), *_rot_pieces(w_kr), _zeros(d, rope_tail),
        w_gate, _zeros(d, HEAD_PAD - FOX_HEADS)], axis=1)
    fox = jnp.pad(w_fox.reshape(d, 3 * FOX_HEADS, FOX_HEAD_DIM),
                  ((0, 0), (0, 0), (0, HEAD_PAD - FOX_HEAD_DIM))).reshape(d, 3 * FOX_HEADS * HEAD_PAD)
    win = jnp.concatenate([w_lat, small, w_pool, fox], axis=1)
    assert win.shape == (d, N_IN_PAD)

    qd = MLA_NOPE + MLA_ROPE
    plain, rot = [], []
    for hh in range(MLA_HEADS):
        plain += [w_q_b[:, hh * qd:(hh + 1) * qd], _zeros(MLA_Q_RANK, rope_tail)]
        rot += [_zeros(MLA_Q_RANK, MLA_NOPE), *_rot_pieces(w_q_b[:, hh * qd + MLA_NOPE:(hh + 1) * qd]),
                _zeros(MLA_Q_RANK, rope_tail)]
    wq = jnp.concatenate(plain + rot, axis=1)

    kd = MLA_NOPE + MLA_V
    keys, vals = [], []
    for hh in range(MLA_HEADS):
        keys += [w_kv_b[:, hh * kd:hh * kd + MLA_NOPE], _zeros(MLA_KV_RANK, HEAD_PAD - MLA_NOPE)]
        vals += [w_kv_b[:, hh * kd + MLA_NOPE:(hh + 1) * kd], _zeros(MLA_KV_RANK, HEAD_PAD - MLA_V)]
    wkv = jnp.concatenate(keys + vals, axis=1)

    ng = len(POOL_WINDOWS)
    pbd = jnp.concatenate([
        jnp.concatenate([pool_w[g] if c == g else _zeros(POOL_GROUP, POOL_GROUP) for c in range(ng)], axis=1)
        for g in range(ng)], axis=0)

    na = MLA_HEADS * MLA_V
    wo = jnp.concatenate([w_out[:na], w_out[na + POOL_WIDTH:]], axis=0)
    wb = w_out[na:na + POOL_WIDTH]
    return win, wq, wkv, pbd, wo, wb


def _gate_lane_constants():
    nf = FOX_HEADS * HEAD_PAD
    e = [[0.0] * (2 * nf) for _ in range(HEAD_PAD)]
    oq = [0.0] * nf
    ok = [0.0] * nf
    ov = [0.0] * nf
    for hh in range(FOX_HEADS):
        ov[hh * HEAD_PAD + V_DIM] = 1.0
        for part in range(3):
            e[part * FOX_HEADS + hh][hh * HEAD_PAD + FOX_HEAD_DIM + part] = 1.0
            e[part * FOX_HEADS + hh][nf + hh * HEAD_PAD + FOX_HEAD_DIM + 3 + part] = -1.0
            oq[hh * HEAD_PAD + FOX_HEAD_DIM + 3 + part] = 1.0
            ok[hh * HEAD_PAD + FOX_HEAD_DIM + part] = 1.0
    row = lambda r: jnp.array(r, F32).reshape(1, nf)
    return jnp.array(e, F32).astype(BF16), row(oq), row(ok), row(ov)


def _rope_tables(seq):
    r = MLA_ROPE
    inv_freq = ROPE_THETA ** (-np.arange(0, r, 2, dtype=np.float64) / r)
    ang = np.arange(seq, dtype=np.float64)[:, None] * inv_freq[None, :]
    cos = np.cos(ang).astype(np.float32)
    sin = np.sin(ang).astype(np.float32)
    tail = HEAD_PAD - MLA_NOPE - MLA_ROPE
    cos_t = np.concatenate([np.ones((seq, MLA_NOPE), np.float32), cos, cos, np.zeros((seq, tail), np.float32)], axis=1)
    sin_t = np.concatenate([np.zeros((seq, MLA_NOPE), np.float32), sin, sin, np.zeros((seq, tail), np.float32)], axis=1)
    return jnp.asarray(cos_t), jnp.asarray(sin_t)


def kernel(x, ffn1_norm, ffn1_w_gu, ffn1_w_down, mix_norm, w_in, q_a_norm, w_q_b, kv_a_norm, w_kv_b, pool_w, pool_scale, fox_b_f, w_out, ffn2_norm, ffn2_w_gu, ffn2_w_down, final_norm):
    batch, seq, d = x.shape
    xf = x.reshape(batch * seq, d)
    cos_t, sin_t = _rope_tables(seq)
    emat, oq, ok, ov = _gate_lane_constants()
    fin = final_norm.reshape(1, d)
    wgu1, wd1 = ffn1_w_gu.astype(BF16), ffn1_w_down.astype(BF16)
    wgu2, wd2 = ffn2_w_gu.astype(BF16), ffn2_w_down.astype(BF16)
    w_q_b_b, w_kv_b_b = w_q_b.astype(BF16), w_kv_b.astype(BF16)
    pool_w_b, w_out_b = pool_w.astype(BF16), w_out.astype(BF16)
    for l in range(DEPTH):
        xf, hmix = _ffn(xf, ffn1_norm[l].reshape(1, d), wgu1, wd1, l, mix_norm[l].reshape(1, d), "next")
        win, wq, wkv, pbd, wo, wb = _prep_mix_weights(w_in[l], w_q_b_b[l], w_kv_b_b[l], pool_w_b[l], w_out_b[l])
        fb = jnp.pad(fox_b_f[l], (0, HEAD_PAD - FOX_HEADS)).reshape(1, HEAD_PAD)
        q, k, v, yb = _mix(hmix, win, q_a_norm[l].reshape(1, -1), wq, kv_a_norm[l].reshape(1, -1), wkv,
                           cos_t, sin_t, pbd, pool_scale[l].reshape(1, -1), fb, emat, oq, ok, ov, batch, seq)
        o = _attention(q, k, v, batch, seq)
        last = l == DEPTH - 1
        xf = _ffn(xf, ffn2_norm[l].reshape(1, d), wgu2, wd2, l, fin, "final" if last else "plain",
                  mix=(o, yb, wo, wb))[0]
    return xf.reshape(batch, seq, d)
```

```python
import functools
import math

import jax
import jax.numpy as jnp
import numpy as np
from jax import lax
from jax.experimental import pallas as pl
from jax.experimental.pallas import tpu as pltpu

F32 = jnp.float32
BF16 = jnp.bfloat16

D_MODEL = 1024
D_FF = 2816
DEPTH = 2
EPS = 1e-6
MLA_HEADS = 6
MLA_Q_RANK = 256
MLA_KV_RANK = 128
MLA_NOPE = 64
MLA_ROPE = 32
MLA_V = 64
ROPE_THETA = 10000.0
POOL_WINDOWS = (2, 4, 8, 16)
POOL_GROUP = 64
POOL_WIDTH = 256
FOX_HEADS = 6
FOX_HEAD_DIM = 64
HEADS = MLA_HEADS + FOX_HEADS
HEAD_PAD = 128
V_DIM = 64
LOG2E = math.log2(math.e)
SCALE_MLA = LOG2E / math.sqrt(MLA_NOPE + MLA_ROPE)
SCALE_FOX = LOG2E / math.sqrt(FOX_HEAD_DIM)
POOL_HISTORY = 16
NEG_BIG = -1e30

C_QA = 0
C_KVA = C_QA + MLA_Q_RANK
C_KR = C_KVA + MLA_KV_RANK
C_KRR = C_KR + HEAD_PAD
C_FF = C_KRR + HEAD_PAD
C_POOL = C_FF + HEAD_PAD
C_FQ = C_POOL + POOL_WIDTH
C_FK = C_FQ + FOX_HEADS * HEAD_PAD
C_FV = C_FK + FOX_HEADS * HEAD_PAD
N_IN_PAD = C_FV + FOX_HEADS * HEAD_PAD

VMEM_LIMIT = 56 * 1024 * 1024

FFN_TM = 512
FFN_FC = 1536
MIX_TS = 512
ATT_T = 512
ATT_RC = 64
ATT_PAIRS = 2


def _rms(x, g):
    return x * lax.rsqrt(jnp.mean(x * x, axis=-1, keepdims=True) + EPS) * g


def _split3(x):
    hi = x.astype(BF16)
    r = x - hi.astype(F32)
    mid = r.astype(BF16)
    lo = (r - mid.astype(F32)).astype(BF16)
    return hi, mid, lo


def _const_spec(shape):
    return pl.BlockSpec(shape, lambda *_: (0,) * len(shape))


def _ffn_kernel(*refs, fc, mode, mixed):
    refs = list(refs)
    x_ref = refs.pop(0)
    if mixed:
        o_ref, yb_ref, wo_ref, wb_ref = refs[:4]
        refs = refs[4:]
    g_ref, wgu_ref, wd_ref, post_ref, out_ref = refs[:5]
    a_ref = refs[-1]
    x = x_ref[...]
    if mixed:
        x = (x + jnp.dot(o_ref[...], wo_ref[...], preferred_element_type=F32)
             + jnp.dot(yb_ref[...], wb_ref[...], preferred_element_type=F32))
    h = _rms(x, g_ref[...]).astype(BF16)
    for lo in range(0, D_FF, fc):
        hi = min(lo + fc, D_FF)
        g = jnp.dot(h, wgu_ref[:, lo:hi].astype(BF16), preferred_element_type=F32)
        u = jnp.dot(h, wgu_ref[:, D_FF + lo:D_FF + hi].astype(BF16), preferred_element_type=F32)
        a_ref[:, lo:hi] = (g * jax.nn.sigmoid(g) * u).astype(BF16)
    y = x + 0.5 * jnp.dot(a_ref[...], wd_ref[...].astype(BF16), preferred_element_type=F32)
    if mode == "final":
        y = _rms(y, post_ref[...])
    out_ref[...] = y
    if mode == "next":
        refs[5][...] = _rms(y, post_ref[...]).astype(BF16)


def _resident(shape, index_map):
    return pl.BlockSpec(shape, index_map, pipeline_mode=pl.Buffered(1))


def _ffn(x, g, wgu, wd, layer, post, mode, mix=None):
    t = x.shape[0]
    tm = FFN_TM
    tile = pl.BlockSpec((tm, D_MODEL), lambda i: (i, 0))
    vec = _resident((1, D_MODEL), lambda i: (0, 0))
    operands, in_specs = [x], [tile]
    if mix is not None:
        o, yb, wo, wb = mix
        operands += [o, yb, wo, wb]
        in_specs += [pl.BlockSpec((tm, o.shape[1]), lambda i: (i, 0)),
                     pl.BlockSpec((tm, yb.shape[1]), lambda i: (i, 0)),
                     _resident(wo.shape, lambda i: (0, 0)),
                     _resident(wb.shape, lambda i: (0, 0))]
    operands += [g, wgu, wd, post]
    in_specs += [vec,
                 _resident((None, D_MODEL, 2 * D_FF), lambda i: (layer, 0, 0)),
                 _resident((None, D_FF, D_MODEL), lambda i: (layer, 0, 0)),
                 vec]
    out_specs = [tile]
    out_shape = [jax.ShapeDtypeStruct((t, D_MODEL), F32)]
    if mode == "next":
        out_specs.append(tile)
        out_shape.append(jax.ShapeDtypeStruct((t, D_MODEL), BF16))
    return pl.pallas_call(
        functools.partial(_ffn_kernel, fc=FFN_FC, mode=mode, mixed=mix is not None),
        grid=(t // tm,),
        in_specs=in_specs,
        out_specs=out_specs,
        out_shape=out_shape,
        scratch_shapes=[pltpu.VMEM((tm, D_FF), BF16)],
        compiler_params=pltpu.CompilerParams(
            dimension_semantics=("arbitrary",), vmem_limit_bytes=VMEM_LIMIT),
        name="ffn",
    )(*operands)


def _mix_kernel(h_ref, win_ref, qan_ref, wq_ref, kvn_ref, wkv_ref, cos_ref, sin_ref,
                pbd_ref, pscale_ref, fb_ref, e_ref, oq_ref, ok_ref, ov_ref,
                q_out, k_out, v_out, yb_out, hist_ref, carry_ref, *, ts):
    si = pl.program_id(1)

    h = h_ref[...]

    @pl.when(si == 0)
    def _():
        zero = h[0:POOL_HISTORY, 0:POOL_WIDTH].astype(F32) * 0.0
        hist_ref[...] = zero
        carry_ref[...] = zero[0:1, 0:HEAD_PAD]

    def proj(lo, hi):
        return jnp.dot(h, win_ref[:, lo:hi], preferred_element_type=F32)

    cos_t = cos_ref[...]
    sin_t = sin_ref[...]
    nq = MLA_HEADS * HEAD_PAD

    z_krr_ff = proj(C_KRR, C_POOL)
    z_kv_kr = proj(C_KVA, C_KRR)
    z_qa = proj(C_QA, C_KVA)
    u = proj(C_POOL, C_FQ)
    nf = FOX_HEADS * HEAD_PAD
    ones_v = ov_ref[...]
    v_out[:, nq:nq + nf] = (proj(C_FV, N_IN_PAD) + ones_v).astype(BF16)
    z_fq = proj(C_FQ, C_FK)
    z_fk = proj(C_FK, C_FV)

    hq = _rms(z_qa, qan_ref[...]).astype(BF16)
    zq = jnp.dot(hq, wq_ref[...], preferred_element_type=F32)
    for hh in range(MLA_HEADS):
        a = zq[:, hh * HEAD_PAD:(hh + 1) * HEAD_PAD]
        b = zq[:, nq + hh * HEAD_PAD:nq + (hh + 1) * HEAD_PAD]
        q_out[:, hh * HEAD_PAD:(hh + 1) * HEAD_PAD] = ((a * cos_t + b * sin_t) * SCALE_MLA).astype(BF16)

    hkv = _rms(z_kv_kr[:, :MLA_KV_RANK], kvn_ref[...]).astype(BF16)
    zkv = jnp.dot(hkv, wkv_ref[...], preferred_element_type=F32)
    kpe = z_kv_kr[:, MLA_KV_RANK:] * cos_t + z_krr_ff[:, :HEAD_PAD] * sin_t
    for hh in range(MLA_HEADS):
        k_out[:, hh * HEAD_PAD:(hh + 1) * HEAD_PAD] = (zkv[:, hh * HEAD_PAD:(hh + 1) * HEAD_PAD] + kpe).astype(BF16)
    v_out[:, 0:nq] = (zkv[:, nq:2 * nq] + ones_v).astype(BF16)

    e = jnp.concatenate([hist_ref[...], u], axis=0)
    s2 = e + pltpu.roll(e, 1, 0)
    s4 = s2 + pltpu.roll(s2, 2, 0)
    s8 = s4 + pltpu.roll(s4, 4, 0)
    s16 = s8 + pltpu.roll(s8, 8, 0)
    hist_ref[...] = u[ts - POOL_HISTORY:, :]
    lane = lax.broadcasted_iota(jnp.int32, (ts, POOL_WIDTH), 1)
    row = lax.broadcasted_iota(jnp.int32, (ts, POOL_WIDTH), 0)
    g_id = lane // POOL_GROUP
    wsum = jnp.where(g_id == 0, s2[POOL_HISTORY:], jnp.where(g_id == 1, s4[POOL_HISTORY:],
                     jnp.where(g_id == 2, s8[POOL_HISTORY:], s16[POOL_HISTORY:])))
    width = jnp.where(g_id == 0, 2, jnp.where(g_id == 1, 4, jnp.where(g_id == 2, 8, 16)))
    count = jnp.minimum(si * ts + row + 1, width).astype(F32)
    pooled = (wsum / count - u).astype(BF16)
    yb = jnp.dot(pooled, pbd_ref[...], preferred_element_type=F32) * pscale_ref[...]
    yb_out[...] = yb.astype(BF16)

    zf = z_krr_ff[:, HEAD_PAD:] + fb_ref[...]
    cum = -(jnp.maximum(-zf, 0.0) + jnp.log1p(jnp.exp(-jnp.abs(zf))))
    g_row = lax.broadcasted_iota(jnp.int32, (ts, HEAD_PAD), 0)
    shift = 1
    while shift < ts:
        cum = cum + jnp.where(g_row >= shift, pltpu.roll(cum, shift, 0), 0.0)
        shift *= 2
    cum = cum + carry_ref[...]
    carry_ref[...] = cum[ts - 1:ts, :]
    g_lane = lax.broadcasted_iota(jnp.int32, (ts, HEAD_PAD), 1)
    c0 = jnp.where(g_lane < FOX_HEADS, cum * LOG2E, 0.0)
    c_rep = c0 + pltpu.roll(c0, FOX_HEADS, 1) + pltpu.roll(c0, 2 * FOX_HEADS, 1)
    c_hi, c_mid, c_lo = _split3(c_rep)
    c_parts = jnp.where(g_lane < FOX_HEADS, c_hi, jnp.where(g_lane < 2 * FOX_HEADS, c_mid, c_lo))
    ce = jnp.dot(c_parts, e_ref[...], preferred_element_type=F32)
    q_out[:, nq:nq + nf] = (z_fq * SCALE_FOX + ce[:, :nf] + oq_ref[...]).astype(BF16)
    k_out[:, nq:nq + nf] = (z_fk + ce[:, nf:] + ok_ref[...]).astype(BF16)


def _mix(h, win, qan, wq, kvn, wkv, cos_t, sin_t, pbd, pscale, fb, emat, oq, ok, ov, batch, seq):
    t = h.shape[0]
    ts = MIX_TS
    ns = seq // ts
    row = lambda b, s: (b * ns + s, 0)
    tab = lambda b, s: (s, 0)
    return pl.pallas_call(
        functools.partial(_mix_kernel, ts=ts),
        grid=(batch, ns),
        in_specs=[
            pl.BlockSpec((ts, D_MODEL), row),
            _const_spec((D_MODEL, N_IN_PAD)),
            _const_spec((1, MLA_Q_RANK)),
            _const_spec(wq.shape),
            _const_spec((1, MLA_KV_RANK)),
            _const_spec(wkv.shape),
            pl.BlockSpec((ts, HEAD_PAD), tab),
            pl.BlockSpec((ts, HEAD_PAD), tab),
            _const_spec(pbd.shape),
            _const_spec((1, POOL_WIDTH)),
            _const_spec((1, HEAD_PAD)),
            _const_spec(emat.shape),
            _const_spec(oq.shape),
            _const_spec(ok.shape),
            _const_spec(ov.shape),
        ],
        out_specs=[
            pl.BlockSpec((ts, HEADS * HEAD_PAD), row),
            pl.BlockSpec((ts, HEADS * HEAD_PAD), row),
            pl.BlockSpec((ts, HEADS * HEAD_PAD), row),
            pl.BlockSpec((ts, POOL_WIDTH), row),
        ],
        out_shape=[
            jax.ShapeDtypeStruct((t, HEADS * HEAD_PAD), BF16),
            jax.ShapeDtypeStruct((t, HEADS * HEAD_PAD), BF16),
            jax.ShapeDtypeStruct((t, HEADS * HEAD_PAD), BF16),
            jax.ShapeDtypeStruct((t, POOL_WIDTH), BF16),
        ],
        scratch_shapes=[pltpu.VMEM((POOL_HISTORY, POOL_WIDTH), F32), pltpu.VMEM((1, HEAD_PAD), F32)],
        compiler_params=pltpu.CompilerParams(
            dimension_semantics=("arbitrary", "arbitrary"), vmem_limit_bytes=VMEM_LIMIT),
        name="mixproj",
    )(h, win, qan, wq, kvn, wkv, cos_t, sin_t, pbd, pscale, fb, emat, oq, ok, ov)


def _attn_kernel(q_ref, k_ref, v_ref, o_ref, s_ref, p_ref, m_ref, al_ref, acc_ref, *, t, rc, nq, npair):
    nkb = t // HEAD_PAD
    nh = 2 * npair
    head_cols = [slice(hh * HEAD_PAD, (hh + 1) * HEAD_PAD) for hh in range(nh)]

    def rows_of(blk):
        return pl.ds(pl.multiple_of(blk * t, t), t)

    half = t // 2

    def scores(i, j, hh, diag):
        dims = (((1,), (1,)), ((), ()))
        if diag:
            q_top = q_ref[pl.ds(pl.multiple_of(i * t, t), half), head_cols[hh]]
            q_bot = q_ref[pl.ds(pl.multiple_of(i * t + half, half), half), head_cols[hh]]
            k_top = k_ref[pl.ds(pl.multiple_of(j * t, t), half), head_cols[hh]]
            s_ref[hh, 0:half, 0:half] = lax.dot_general(q_top, k_top, dims, preferred_element_type=F32)
            s_ref[hh, half:t, :] = lax.dot_general(q_bot, k_ref[rows_of(j), head_cols[hh]], dims,
                                                   preferred_element_type=F32)
        else:
            s_ref[hh] = lax.dot_general(q_ref[rows_of(i), head_cols[hh]], k_ref[rows_of(j), head_cols[hh]],
                                        dims, preferred_element_type=F32)

    def softmax(hh, masked, first):
        for c in range(t // rc):
            r0 = c * rc
            rows = slice(r0, r0 + rc)
            live = (r0 + rc - 1) // HEAD_PAD + 1 if masked else nkb
            blocks = []
            for kb in range(live):
                blk = s_ref[hh, rows, kb * HEAD_PAD:(kb + 1) * HEAD_PAD]
                if masked and (kb + 1) * HEAD_PAD - 1 > r0:
                    r_i = r0 + lax.broadcasted_iota(jnp.int32, (rc, HEAD_PAD), 0)
                    c_i = kb * HEAD_PAD + lax.broadcasted_iota(jnp.int32, (rc, HEAD_PAD), 1)
                    blk = jnp.where(c_i <= r_i, blk, NEG_BIG)
                blocks.append(blk)
            bmax = blocks[0]
            for blk in blocks[1:]:
                bmax = jnp.maximum(bmax, blk)
            rmax = jnp.max(bmax, axis=-1, keepdims=True)
            m_old = jnp.full((rc, HEAD_PAD), NEG_BIG, F32) if first else m_ref[hh, rows, :]
            m_new = jnp.maximum(m_old, rmax)
            al_ref[hh, rows, :] = jnp.exp2(m_old - m_new)
            m_ref[hh, rows, :] = m_new
            for kb in range(nkb):
                pcols = slice(kb * HEAD_PAD, (kb + 1) * HEAD_PAD)
                if kb < live:
                    p_ref[hh, rows, pcols] = jnp.exp2(blocks[kb] - m_new).astype(BF16)
                elif r0 >= half or (kb + 1) * HEAD_PAD <= half:
                    p_ref[hh, rows, pcols] = (blocks[0] * 0.0).astype(BF16)

    def accumulate(j, hh, diag):
        if diag:
            v_top = v_ref[pl.ds(pl.multiple_of(j * t, t), half), head_cols[hh]]
            pv_top = jnp.dot(p_ref[hh, 0:half, 0:half], v_top, preferred_element_type=F32)
            acc_ref[hh, 0:half, :] = al_ref[hh, 0:half, :] * acc_ref[hh, 0:half, :] + pv_top
            pv_bot = jnp.dot(p_ref[hh, half:t, :], v_ref[rows_of(j), head_cols[hh]], preferred_element_type=F32)
            acc_ref[hh, half:t, :] = al_ref[hh, half:t, :] * acc_ref[hh, half:t, :] + pv_bot
        else:
            pv = jnp.dot(p_ref[hh], v_ref[rows_of(j), head_cols[hh]], preferred_element_type=F32)
            acc_ref[hh] = al_ref[hh] * acc_ref[hh] + pv

    def finish(i):
        outs = []
        for hh in range(nh):
            acc = acc_ref[hh]
            outs.append(acc[:, :V_DIM] / acc[:, V_DIM:V_DIM + 1])
        o_ref[rows_of(i), :] = jnp.concatenate(outs, axis=1).astype(BF16)

    for hh in range(nh):
        zero = v_ref[0:t, head_cols[hh]].astype(F32) * 0.0
        acc_ref[hh] = zero
        if hh % 2 == 1:
            m_ref[hh] = zero + NEG_BIG
    for ha in range(0, nh, 2):
        scores(0, 0, ha, True)
        scores(0, 0, ha + 1, True)
        softmax(ha, True, True)

    def stage(i, j, ni, nj, cur_diag, next_diag):
        for ha in range(0, nh, 2):
            hb = ha + 1
            accumulate(j, ha, cur_diag)
            scores(ni, nj, ha, next_diag)
            softmax(hb, cur_diag, False)
            scores(ni, nj, hb, next_diag)
            softmax(ha, next_diag, cur_diag)
            accumulate(j, hb, cur_diag)
        if cur_diag:
            finish(i)
            for hb in range(1, nh, 2):
                m_ref[hb] = m_ref[hb] * 0.0 + NEG_BIG

    def body(n, carry):
        i, j = carry

        @pl.when(j == i)
        def _():
            stage(i, j, i + 1, 0, True, False)

        @pl.when(j + 1 == i)
        def _():
            stage(i, j, i, j + 1, False, True)

        @pl.when(j + 2 == i)
        def _():
            stage(i, j, i, j + 1, False, False)

        @pl.when(j + 2 < i)
        def _():
            stage(i, j, i, j + 1, False, False)
            stage(i, j + 1, i, j + 2, False, False)

        ni = jnp.where(j == i, i + 1, i)
        nj = jnp.where(j == i, 0, jnp.where(j + 2 < i, j + 2, j + 1))
        return ni, nj

    n_iter, wi, wj = 0, 0, 0
    while (wi, wj) != (nq - 1, nq - 1):
        wi, wj = (wi + 1, 0) if wj == wi else (wi, wj + 2 if wj + 2 < wi else wj + 1)
        n_iter += 1
    last_i, last_j = lax.fori_loop(0, n_iter, body, (jnp.int32(0), jnp.int32(0)))
    for ha in range(0, nh, 2):
        softmax(ha + 1, True, False)
        accumulate(last_j, ha, True)
        accumulate(last_j, ha + 1, True)
    finish(last_i)


def _attention(q, k, v, batch, seq):
    t = ATT_T
    nq = seq // t
    nh = 2 * ATT_PAIRS
    whole = lambda b, p: (b, p)
    return pl.pallas_call(
        functools.partial(_attn_kernel, t=t, rc=ATT_RC, nq=nq, npair=ATT_PAIRS),
        grid=(batch, HEADS // nh),
        in_specs=[
            pl.BlockSpec((seq, nh * HEAD_PAD), whole),
            pl.BlockSpec((seq, nh * HEAD_PAD), whole),
            pl.BlockSpec((seq, nh * HEAD_PAD), whole),
        ],
        out_specs=pl.BlockSpec((seq, nh * V_DIM), whole),
        out_shape=jax.ShapeDtypeStruct((batch * seq, HEADS * V_DIM), BF16),
        scratch_shapes=[
            pltpu.VMEM((nh, t, t), F32),
            pltpu.VMEM((nh, t, t), BF16),
            pltpu.VMEM((nh, t, HEAD_PAD), F32),
            pltpu.VMEM((nh, t, HEAD_PAD), F32),
            pltpu.VMEM((nh, t, HEAD_PAD), F32),
        ],
        compiler_params=pltpu.CompilerParams(
            dimension_semantics=("arbitrary", "arbitrary"), vmem_limit_bytes=VMEM_LIMIT),
        name="attention",
    )(q, k, v)


def _zeros(rows, cols):
    return jnp.zeros((rows, cols), BF16)


def _rot_pieces(w):
    half = w.shape[-1] // 2
    return [-w[:, half:], w[:, :half]]


def _prep_mix_weights(w_in, w_q_b, w_kv_b, pool_w, w_out):
    o2 = MLA_Q_RANK + MLA_KV_RANK
    o3 = o2 + MLA_ROPE
    o4 = o3 + POOL_WIDTH
    hd = FOX_HEADS * FOX_HEAD_DIM
    o5 = o4 + 3 * hd
    d = w_in.shape[0]
    rope_tail = HEAD_PAD - MLA_NOPE - MLA_ROPE
    w_lat, w_kr, w_pool, w_fox, w_gate = (w_in[:, lo:hi] for lo, hi in
                                          ((0, o2), (o2, o3), (o3, o4), (o4, o5), (o5, None)))
    small = jnp.concatenate([
        _zeros(d, MLA_NOPE), w_kr, _zeros(d, rope_tail),
        _zeros(d, MLA_NOPE), *_rot_pieces(w_kr), _zeros(d, rope_tail),
        w_gate, _zeros(d, HEAD_PAD - FOX_HEADS)], axis=1)
    fox = jnp.pad(w_fox.reshape(d, 3 * FOX_HEADS, FOX_HEAD_DIM),
                  ((0, 0), (0, 0), (0, HEAD_PAD - FOX_HEAD_DIM))).reshape(d, 3 * FOX_HEADS * HEAD_PAD)
    win = jnp.concatenate([w_lat, small, w_pool, fox], axis=1)
    assert win.shape == (d, N_IN_PAD)

    qd = MLA_NOPE + MLA_ROPE
    plain, rot = [], []
    for hh in range(MLA_HEADS):
        plain += [w_q_b[:, hh * qd:(hh + 1) * qd], _zeros(MLA_Q_RANK, rope_tail)]
        rot += [_zeros(MLA_Q_RANK, MLA_NOPE), *_rot_pieces(w_q_b[:, hh * qd + MLA_NOPE:(hh + 1) * qd]),
                _zeros(MLA_Q_RANK, rope_tail)]
    wq = jnp.concatenate(plain + rot, axis=1)

    kd = MLA_NOPE + MLA_V
    keys, vals = [], []
    for hh in range(MLA_HEADS):
        keys += [w_kv_b[:, hh * kd:hh * kd + MLA_NOPE], _zeros(MLA_KV_RANK, HEAD_PAD - MLA_NOPE)]
        vals += [w_kv_b[:, hh * kd + MLA_NOPE:(hh + 1) * kd], _zeros(MLA_KV_RANK, HEAD_PAD - MLA_V)]
    wkv = jnp.concatenate(keys + vals, axis=1)

    ng = len(POOL_WINDOWS)
    pbd = jnp.concatenate([
        jnp.concatenate([pool_w[g] if c == g else _zeros(POOL_GROUP, POOL_GROUP) for c in range(ng)], axis=1)
        for g in range(ng)], axis=0)

    na = MLA_HEADS * MLA_V
    wo = jnp.concatenate([w_out[:na], w_out[na + POOL_WIDTH:]], axis=0)
    wb = w_out[na:na + POOL_WIDTH]
    return win, wq, wkv, pbd, wo, wb


def _gate_lane_constants():
    nf = FOX_HEADS * HEAD_PAD
    e = [[0.0] * (2 * nf) for _ in range(HEAD_PAD)]
    oq = [0.0] * nf
    ok = [0.0] * nf
    ov = [0.0] * nf
    for hh in range(FOX_HEADS):
        ov[hh * HEAD_PAD + V_DIM] = 1.0
        for part in range(3):
            e[part * FOX_HEADS + hh][hh * HEAD_PAD + FOX_HEAD_DIM + part] = 1.0
            e[part * FOX_HEADS + hh][nf + hh * HEAD_PAD + FOX_HEAD_DIM + 3 + part] = -1.0
            oq[hh * HEAD_PAD + FOX_HEAD_DIM + 3 + part] = 1.0
            ok[hh * HEAD_PAD + FOX_HEAD_DIM + part] = 1.0
    row = lambda r: jnp.array(r, F32).reshape(1, nf)
    return jnp.array(e, F32).astype(BF16), row(oq), row(ok), row(ov)


def _rope_tables(seq):
    r = MLA_ROPE
    inv_freq = ROPE_THETA ** (-np.arange(0, r, 2, dtype=np.float64) / r)
    ang = np.arange(seq, dtype=np.float64)[:, None] * inv_freq[None, :]
    cos = np.cos(ang).astype(np.float32)
    sin = np.sin(ang).astype(np.float32)
    tail = HEAD_PAD - MLA_NOPE - MLA_ROPE
    cos_t = np.concatenate([np.ones((seq, MLA_NOPE), np.float32), cos, cos, np.zeros((seq, tail), np.float32)], axis=1)
    sin_t = np.concatenate([np.zeros((seq, MLA_NOPE), np.float32), sin, sin, np.zeros((seq, tail), np.float32)], axis=1)
    return jnp.asarray(cos_t), jnp.asarray(sin_t)


def kernel(x, ffn1_norm, ffn1_w_gu, ffn1_w_down, mix_norm, w_in, q_a_norm, w_q_b, kv_a_norm, w_kv_b, pool_w, pool_scale, fox_b_f, w_out, ffn2_norm, ffn2_w_gu, ffn2_w_down, final_norm):
    batch, seq, d = x.shape
    xf = x.reshape(batch * seq, d)
    cos_t, sin_t = _rope_tables(seq)
    emat, oq, ok, ov = _gate_lane_constants()
    fin = final_norm.reshape(1, d)
    wgu1, wd1 = ffn1_w_gu, ffn1_w_down
    wgu2, wd2 = ffn2_w_gu, ffn2_w_down
    w_in_b, w_q_b_b, w_kv_b_b = w_in.astype(BF16), w_q_b.astype(BF16), w_kv_b.astype(BF16)
    pool_w_b, w_out_b = pool_w.astype(BF16), w_out.astype(BF16)
    for l in range(DEPTH):
        xf, hmix = _ffn(xf, ffn1_norm[l].reshape(1, d), wgu1, wd1, l, mix_norm[l].reshape(1, d), "next")
        win, wq, wkv, pbd, wo, wb = _prep_mix_weights(w_in_b[l], w_q_b_b[l], w_kv_b_b[l], pool_w_b[l], w_out_b[l])
        fb = jnp.pad(fox_b_f[l], (0, HEAD_PAD - FOX_HEADS)).reshape(1, HEAD_PAD)
        q, k, v, yb = _mix(hmix, win, q_a_norm[l].reshape(1, -1), wq, kv_a_norm[l].reshape(1, -1), wkv,
                           cos_t, sin_t, pbd, pool_scale[l].reshape(1, -1), fb, emat, oq, ok, ov, batch, seq)
        o = _attention(q, k, v, batch, seq)
        last = l == DEPTH - 1
        xf = _ffn(xf, ffn2_norm[l].reshape(1, d), wgu2, wd2, l, fin, "final" if last else "plain",
                  mix=(o, yb, wo, wb))[0]
    return xf.reshape(batch, seq, d)
```
